```python
import jax, jax.numpy as jnp
from jax import lax
import numpy as np

D_MODEL = 1024
BATCH = 8
SEQ = 4096
DEPTH = 2

N_EVEN = (DEPTH + 1) // 2
N_ODD = DEPTH // 2
NORM_EPS = 1e-6

RG_WIDTH = D_MODEL
RG_HEADS = 8
RG_HEAD_DIM = RG_WIDTH // RG_HEADS
RG_CONV = 4
RG_C = 8.0
RG_CONV_LEFT = 2
SC_WIDTH = D_MODEL
SC_CONV = 3
SC_CONV_LEFT = 1
EVEN_IN = 2 * RG_WIDTH + 4 * SC_WIDTH
EVEN_MIX = RG_WIDTH + SC_WIDTH
GLA_HEADS = 4
GLA_KEY = D_MODEL // 2
GLA_VAL = D_MODEL
GLA_DK = GLA_KEY // GLA_HEADS
GLA_DV = GLA_VAL // GLA_HEADS
GLA_RANK = 16
GLA_NORMALIZER = 16.0
GLA_CHUNK = 64
ODD_IN = 2 * GLA_KEY + 2 * GLA_VAL + 2 * GLA_RANK

kernel_name = 'hybrid_rglru_shortconv_gla_encoder'


def rmsnorm(x, g):
    xf = x.astype(jnp.float32)
    y = xf * lax.rsqrt(jnp.mean(xf * xf, axis=-1, keepdims=True) + NORM_EPS)
    return (y * g.astype(jnp.float32)).astype(x.dtype)


def centred_dwconv(u, w, left):
    seq = u.shape[1]
    width = w.shape[0]
    up = jnp.pad(u, ((0, 0), (left, width - 1 - left), (0, 0)))
    out = up[:, 0:seq] * w[0]
    for k in range(1, width):
        out = out + up[:, k:k + seq] * w[k]
    return out


def rg_lru(u, gate_w, gate_b, lam, reverse):
    bsz, seq, width = u.shape
    uf = u.astype(jnp.float32)
    uh = uf.reshape(bsz, seq, RG_HEADS, RG_HEAD_DIM)
    gates = jax.nn.sigmoid(
        jnp.einsum('bshi,ghij->gbshj', uh, gate_w.astype(jnp.float32))
        + gate_b[:, None, None].astype(jnp.float32))
    r = gates[0].reshape(bsz, seq, width)
    i = gates[1].reshape(bsz, seq, width)
    log_a = -RG_C * r * jax.nn.softplus(-lam.astype(jnp.float32))
    a = jnp.exp(log_a)
    b = jnp.sqrt(-jnp.expm1(2.0 * log_a)) * (i * uf)

    def combine(left, right):
        a_l, b_l = left
        a_r, b_r = right
        return (a_l * a_r, a_r * b_l + b_r)

    _, h = lax.associative_scan(combine, (a, b), reverse=reverse, axis=1)
    return h


def gla_chunked(q, k, v, g):
    bsz, nh, seq, dk = q.shape
    dv = v.shape[-1]
    nc = seq // GLA_CHUNK
    qc = q.astype(jnp.float32).reshape(bsz, nh, nc, GLA_CHUNK, dk) * (dk ** -0.5)
    kc = k.astype(jnp.float32).reshape(bsz, nh, nc, GLA_CHUNK, dk)
    vc = v.astype(jnp.float32).reshape(bsz, nh, nc, GLA_CHUNK, dv)
    gc = g.reshape(bsz, nh, nc, GLA_CHUNK, dk)
    bcum = jnp.cumsum(gc, axis=3)
    btot = bcum[..., -1:, :]
    q_in = qc * jnp.exp(bcum)
    k_in = kc * jnp.exp(-bcum)
    k_st = kc * jnp.exp(btot - bcum)
    mask = jnp.tril(jnp.ones((GLA_CHUNK, GLA_CHUNK), dtype=bool))
    scores = jnp.einsum('bhnid,bhnjd->bhnij', q_in, k_in)
    scores = jnp.where(mask, scores, 0.0)
    o_intra = jnp.einsum('bhnij,bhnje->bhnie', scores, vc)
    decay = jnp.exp(btot[..., 0, :])

    def step(state, xs):
        q_n, k_n, v_n, dec_n = xs
        o_n = jnp.einsum('bhcd,bhde->bhce', q_n, state)
        state = dec_n[..., None] * state + jnp.einsum('bhcd,bhce->bhde', k_n, v_n)
        return state, o_n

    xs = (jnp.moveaxis(q_in, 2, 0), jnp.moveaxis(k_st, 2, 0),
          jnp.moveaxis(vc, 2, 0), jnp.moveaxis(decay, 2, 0))
    state0 = jnp.zeros((bsz, nh, dk, dv), jnp.float32)
    _, o_inter = lax.scan(step, state0, xs)
    o_inter = jnp.moveaxis(o_inter, 0, 2)
    return (o_intra + o_inter).reshape(bsz, nh, seq, dv)


def even_layer(x, norm_pre, norm_post, w_in, conv_w, conv_b, gate_w, gate_b, lam, sc_w, w_out):
    h = rmsnorm(x, norm_pre)
    proj = h @ w_in
    xa, za, xb, gb, gc, zb = jnp.split(
        proj, [RG_WIDTH, 2 * RG_WIDTH, 2 * RG_WIDTH + SC_WIDTH,
               2 * RG_WIDTH + 2 * SC_WIDTH, 2 * RG_WIDTH + 3 * SC_WIDTH], axis=-1)
    ua = centred_dwconv(xa, conv_w, RG_CONV_LEFT) + conv_b
    ya = rg_lru(ua, gate_w[0], gate_b[0], lam[0], False) + rg_lru(ua, gate_w[1], gate_b[1], lam[1], True)
    ya = ya * jax.nn.silu(za.astype(jnp.float32))
    yb = gb * centred_dwconv(gc * xb, sc_w, SC_CONV_LEFT)
    yb = yb * jax.nn.silu(zb)
    y = jnp.concatenate([ya, yb.astype(jnp.float32)], axis=-1) @ w_out
    return (x + rmsnorm(y, norm_post)).astype(x.dtype)


def odd_layer(x, norm_pre, norm_post, w_in, w_gate_lr, b_gate, head_norm_g, w_out):
    bsz, seq, _ = x.shape
    h = rmsnorm(x, norm_pre)
    proj = h @ w_in
    q, k, v, r, lr = jnp.split(
        proj, [GLA_KEY, 2 * GLA_KEY, 2 * GLA_KEY + GLA_VAL, 2 * GLA_KEY + 2 * GLA_VAL], axis=-1)
    lr = lr.reshape(bsz, seq, 2, GLA_RANK).astype(jnp.float32)
    z = jnp.einsum('bsdr,drk->dbsk', lr, w_gate_lr.astype(jnp.float32)) + b_gate[:, None, None].astype(jnp.float32)
    log_alpha = jax.nn.log_sigmoid(z) / GLA_NORMALIZER

    def heads(t, dh):
        return jnp.transpose(t.reshape(bsz, seq, GLA_HEADS, dh), (0, 2, 1, 3))

    qh, kh, vh = heads(q, GLA_DK), heads(k, GLA_DK), heads(v, GLA_DV)
    g_f, g_b = heads(log_alpha[0], GLA_DK), heads(log_alpha[1], GLA_DK)
    o_f = gla_chunked(qh, kh, vh, g_f)
    o_b = jnp.flip(gla_chunked(jnp.flip(qh, 2), jnp.flip(kh, 2), jnp.flip(vh, 2), jnp.flip(g_b, 2)), 2)
    o = rmsnorm(o_f + o_b, head_norm_g)
    o = jnp.transpose(o, (0, 2, 1, 3)).reshape(bsz, seq, GLA_VAL)
    y = (o * jax.nn.silu(r.astype(jnp.float32))) @ w_out
    return (x + rmsnorm(y, norm_post)).astype(x.dtype)


def setup_inputs(seed: int = 0) -> dict:
    key = jax.random.key(seed)
    ks = jax.random.split(key, 20)
    nrm = jax.random.normal
    f32 = jnp.float32
    lam_a = jax.random.uniform(ks[8], (N_EVEN, 2, RG_WIDTH), f32, minval=0.9, maxval=0.999)
    lam_s = lam_a ** (1.0 / RG_C)
    rg_lambda = jnp.log(lam_s) - jnp.log1p(-lam_s)
    return {
        'x': nrm(ks[0], (BATCH, SEQ, D_MODEL), f32),
        'even_norm_pre': 1.0 + 0.05 * nrm(ks[1], (N_EVEN, D_MODEL), f32),
        'even_norm_post': 1.0 + 0.05 * nrm(ks[2], (N_EVEN, D_MODEL), f32),
        'even_w_in': nrm(ks[3], (N_EVEN, D_MODEL, EVEN_IN), f32) * D_MODEL ** -0.5,
        'rg_conv_w': nrm(ks[4], (N_EVEN, RG_CONV, RG_WIDTH), f32) * RG_CONV ** -0.5,
        'rg_conv_b': 0.01 * nrm(ks[5], (N_EVEN, RG_WIDTH), f32),
        'rg_gate_w': nrm(ks[6], (N_EVEN, 2, 2, RG_HEADS, RG_HEAD_DIM, RG_HEAD_DIM), f32) * RG_HEAD_DIM ** -0.5,
        'rg_gate_b': 0.01 * nrm(ks[7], (N_EVEN, 2, 2, RG_HEADS, RG_HEAD_DIM), f32),
        'rg_lambda': rg_lambda,
        'sc_conv_w': nrm(ks[9], (N_EVEN, SC_CONV, SC_WIDTH), f32) * SC_CONV ** -0.5,
        'even_w_out': nrm(ks[10], (N_EVEN, EVEN_MIX, D_MODEL), f32) * EVEN_MIX ** -0.5,
        'odd_norm_pre': 1.0 + 0.05 * nrm(ks[11], (N_ODD, D_MODEL), f32),
        'odd_norm_post': 1.0 + 0.05 * nrm(ks[12], (N_ODD, D_MODEL), f32),
        'odd_w_in': nrm(ks[13], (N_ODD, D_MODEL, ODD_IN), f32) * D_MODEL ** -0.5,
        'gla_w_gate_lr': nrm(ks[14], (N_ODD, 2, GLA_RANK, GLA_KEY), f32) * GLA_RANK ** -0.5,
        'gla_b_gate': 1.0 + 0.3 * nrm(ks[15], (N_ODD, 2, GLA_KEY), f32),
        'gla_norm_g': 1.0 + 0.05 * nrm(ks[16], (N_ODD, GLA_DV), f32),
        'odd_w_out': nrm(ks[17], (N_ODD, GLA_VAL, D_MODEL), f32) * GLA_VAL ** -0.5,
    }


def reference(x, even_norm_pre, even_norm_post, even_w_in, rg_conv_w, rg_conv_b, rg_gate_w,
              rg_gate_b, rg_lambda, sc_conv_w, even_w_out, odd_norm_pre, odd_norm_post,
              odd_w_in, gla_w_gate_lr, gla_b_gate, gla_norm_g, odd_w_out):
    for layer in range(DEPTH):
        j = layer // 2
        if layer % 2 == 0:
            x = even_layer(x, even_norm_pre[j], even_norm_post[j], even_w_in[j], rg_conv_w[j],
                           rg_conv_b[j], rg_gate_w[j], rg_gate_b[j], rg_lambda[j],
                           sc_conv_w[j], even_w_out[j])
        else:
            x = odd_layer(x, odd_norm_pre[j], odd_norm_post[j], odd_w_in[j], gla_w_gate_lr[j],
                          gla_b_gate[j], gla_norm_g[j], odd_w_out[j])
    return x
```

```python
import functools

import jax
import jax.numpy as jnp
from jax import lax
from jax.experimental import pallas as pl
from jax.experimental.pallas import tpu as pltpu

F32 = jnp.float32
BF16 = jnp.bfloat16

NORM_EPS = 1e-6
RG_HEADS = 8
RG_C = 8.0
GLA_HEADS = 4
GLA_RANK = 16
GLA_NORMALIZER = 16.0
GLA_CHUNK = 64

SUBLANES = 8
LANES = 128
HALO = 16
EVEN_TILE = 256
ODD_TILE = 512
GLA_BLOCK = 256
VMEM_LIMIT = 56 * 1024 * 1024


def _rms_scale(x):
    return lax.rsqrt(jnp.mean(x * x, axis=-1, keepdims=True) + NORM_EPS)


def _silu(x):
    return x * jax.nn.sigmoid(x)


def _dot(a, b):
    return jnp.dot(a, b, preferred_element_type=F32)


def _slab_scan(a, b, reverse):
    t, c = a.shape
    a3 = a.reshape(t // SUBLANES, SUBLANES, c)
    b3 = b.reshape(t // SUBLANES, SUBLANES, c)
    row = lax.broadcasted_iota(jnp.int32, a3.shape, 1)
    for step in (1, 2, 4):
        if reverse:
            shift = SUBLANES - step
            valid = row < SUBLANES - step
        else:
            shift = step
            valid = row >= step
        a_prev = pltpu.roll(a3, shift, 1)
        b_prev = pltpu.roll(b3, shift, 1)
        am = jnp.where(valid, a3, 0.0)
        b3 = am * b_prev + b3
        a3 = jnp.where(valid, a3 * a_prev, a3)
    return a3.reshape(t, c), b3.reshape(t, c)


def _even_fwd_kernel(xp_ref, x_ref, xn_ref, gpre_ref, win_ref, cw_ref, cb_ref, gw_ref, gb_ref,
                     lam_ref, scw_ref, wout_ref,
                     yp_ref, pz_ref, summ_ref,
                     hbuf, ext_a, ext_b, ua_buf, pf_buf, hf_buf, pb_buf, hb_buf, ycat, carry_f):
    t = pl.program_id(1)
    nt = pl.num_programs(1)
    tile = x_ref.shape[0]
    width = x_ref.shape[1]
    hd = width // RG_HEADS
    nslab = tile // SUBLANES

    gpre = gpre_ref[...]
    for src, lo, n in ((xp_ref, 0, HALO), (x_ref, HALO, tile), (xn_ref, HALO + tile, HALO)):
        xs = src[...]
        hbuf[lo:lo + n, :] = (xs * _rms_scale(xs) * gpre).astype(BF16)

    top_ok = (t > 0).astype(F32)
    bot_ok = (t < nt - 1).astype(F32)

    def halo_mask(ref):
        ref[0:HALO, :] = ref[0:HALO, :] * top_ok
        ref[HALO + tile:, :] = ref[HALO + tile:, :] * bot_ok

    ext_a[...] = _dot(hbuf[...], win_ref[:, 0:width])
    halo_mask(ext_a)
    cw = cw_ref[...]
    ua = cb_ref[...] + cw[0:1, :] * ext_a[HALO - 2:HALO - 2 + tile, :]
    for k in range(1, 4):
        ua = ua + cw[k:k + 1, :] * ext_a[HALO - 2 + k:HALO - 2 + k + tile, :]
    ua_buf[...] = ua

    lam = lam_ref[...]
    log_scale = -RG_C * (jnp.maximum(-lam, 0.0) + jnp.log1p(jnp.exp(-jnp.abs(lam))))
    gb = gb_ref[...]
    for h in range(RG_HEADS):
        cs = slice(h * hd, (h + 1) * hd)
        u = ua_buf[:, cs]
        g = _dot(u.astype(BF16), gw_ref[h])
        for d, (p_buf, h_buf) in enumerate(((pf_buf, hf_buf), (pb_buf, hb_buf))):
            r = jax.nn.sigmoid(g[:, (2 * d) * hd:(2 * d + 1) * hd] + gb[2 * d:2 * d + 1, cs])
            i = jax.nn.sigmoid(g[:, (2 * d + 1) * hd:(2 * d + 2) * hd] + gb[2 * d + 1:2 * d + 2, cs])
            a = jnp.exp(log_scale[d:d + 1, cs] * r)
            b = jnp.sqrt(1.0 - a * a) * (i * u)
            p_loc, h_loc = _slab_scan(a, b, reverse=(d == 1))
            p_buf[:, cs] = p_loc
            h_buf[:, cs] = h_loc

    @pl.when(t == 0)
    def _():
        carry_f[...] = jnp.zeros_like(carry_f)

    def fwd_body(s, c):
        rows = pl.ds(pl.multiple_of(s * SUBLANES, SUBLANES), SUBLANES)
        h = hf_buf[rows, :] + pf_buf[rows, :] * c
        hf_buf[rows, :] = h
        return jnp.broadcast_to(h[SUBLANES - 1:SUBLANES, :], h.shape)

    carry_f[...] = lax.fori_loop(0, nslab, fwd_body, carry_f[...], unroll=4)

    def bwd_body(k, carry):
        c, p = carry
        s = nslab - 1 - k
        rows = pl.ds(pl.multiple_of(s * SUBLANES, SUBLANES), SUBLANES)
        p_loc = pb_buf[rows, :]
        h = hb_buf[rows, :] + p_loc * c
        pp = p_loc * p
        hb_buf[rows, :] = h
        pb_buf[rows, :] = pp
        return (jnp.broadcast_to(h[0:1, :], h.shape), jnp.broadcast_to(pp[0:1, :], pp.shape))

    zeros = jnp.zeros((SUBLANES, width), F32)
    lax.fori_loop(0, nslab, bwd_body, (zeros, zeros + 1.0), unroll=4)
    summ_ref[0:1, :] = hb_buf[0:1, :]
    summ_ref[1:2, :] = pb_buf[0:1, :]

    sza = _silu(_dot(hbuf[HALO:HALO + tile, :], win_ref[:, width:2 * width]))
    ycat[:, 0:width] = ((hf_buf[...] + hb_buf[...]) * sza).astype(BF16)
    pz_ref[...] = (pb_buf[...] * sza).astype(BF16)

    ext_a[...] = _dot(hbuf[...], win_ref[:, 2 * width:3 * width])
    ext_b[...] = _dot(hbuf[...], win_ref[:, 4 * width:5 * width])
    ext_a[...] = ext_a[...] * ext_b[...]
    halo_mask(ext_a)
    scw = scw_ref[...]
    cv = scw[0:1, :] * ext_a[HALO - 1:HALO - 1 + tile, :]
    for k in range(1, 3):
        cv = cv + scw[k:k + 1, :] * ext_a[HALO - 1 + k:HALO - 1 + k + tile, :]
    hcur = hbuf[HALO:HALO + tile, :]
    gate_b = _dot(hcur, win_ref[:, 3 * width:4 * width])
    zb = _dot(hcur, win_ref[:, 5 * width:6 * width])
    ycat[:, width:2 * width] = (gate_b * cv * _silu(zb)).astype(BF16)

    yp_ref[...] = _dot(ycat[...], wout_ref[...])


def _even_fwd(x, gpre, w_in, conv_w, conv_b, gate_w, gate_b, lam, sc_w, w_out):
    bsz, seq, width = x.shape
    tile = EVEN_TILE
    nt = seq // tile
    hpt = tile // HALO
    nh = seq // HALO
    const = lambda shape: pl.BlockSpec(shape, lambda b, t: (0,) * len(shape))
    ext = tile + 2 * HALO
    return pl.pallas_call(
        _even_fwd_kernel,
        grid=(bsz, nt),
        in_specs=[
            pl.BlockSpec((None, HALO, width), lambda b, t: (b, jnp.maximum(t * hpt - 1, 0), 0)),
            pl.BlockSpec((None, tile, width), lambda b, t: (b, t, 0)),
            pl.BlockSpec((None, HALO, width), lambda b, t: (b, jnp.minimum((t + 1) * hpt, nh - 1), 0)),
            const(gpre.shape), const(w_in.shape), const(conv_w.shape), const(conv_b.shape),
            const(gate_w.shape), const(gate_b.shape), const(lam.shape), const(sc_w.shape),
            const(w_out.shape),
        ],
        out_specs=[
            pl.BlockSpec((None, tile, width), lambda b, t: (b, t, 0)),
            pl.BlockSpec((None, tile, width), lambda b, t: (b, t, 0)),
            pl.BlockSpec((None, None, 2, width), lambda b, t: (b, t, 0, 0)),
        ],
        out_shape=[
            jax.ShapeDtypeStruct((bsz, seq, width), F32),
            jax.ShapeDtypeStruct((bsz, seq, width), BF16),
            jax.ShapeDtypeStruct((bsz, nt, 2, width), F32),
        ],
        scratch_shapes=[
            pltpu.VMEM((ext, width), BF16),
            pltpu.VMEM((ext, width), F32),
            pltpu.VMEM((ext, width), F32),
            pltpu.VMEM((tile, width), F32),
            pltpu.VMEM((tile, width), F32),
            pltpu.VMEM((tile, width), F32),
            pltpu.VMEM((tile, width), F32),
            pltpu.VMEM((tile, width), F32),
            pltpu.VMEM((tile, 2 * width), BF16),
            pltpu.VMEM((SUBLANES, width), F32),
        ],
        compiler_params=pltpu.CompilerParams(
            dimension_semantics=("arbitrary", "arbitrary"), vmem_limit_bytes=VMEM_LIMIT),
        name="even_fwd",
    )(x, x, x, gpre, w_in, conv_w, conv_b, gate_w, gate_b, lam, sc_w, w_out)


def _even_fix_kernel(x_ref, yp_ref, pz_ref, summ_ref, w1_ref, gpost_ref, out_ref, carry_b):
    k = pl.program_id(1)

    @pl.when(k == 0)
    def _():
        carry_b[...] = jnp.zeros_like(carry_b)

    c = carry_b[...]
    lhs = (pz_ref[...].astype(F32) * c).astype(BF16)
    y = yp_ref[...] + _dot(lhs, w1_ref[...])
    out_ref[...] = x_ref[...] + y * _rms_scale(y) * gpost_ref[...]
    carry_b[...] = summ_ref[0:1, :] + summ_ref[1:2, :] * c


def _even_fix(x, yp, pz, summ, w1, gpost):
    bsz, seq, width = x.shape
    tile = EVEN_TILE
    nt = seq // tile
    rev = lambda b, k: (b, nt - 1 - k, 0)
    const = lambda shape: pl.BlockSpec(shape, lambda b, k: (0,) * len(shape))
    return pl.pallas_call(
        _even_fix_kernel,
        grid=(bsz, nt),
        in_specs=[
            pl.BlockSpec((None, tile, width), rev),
            pl.BlockSpec((None, tile, width), rev),
            pl.BlockSpec((None, tile, width), rev),
            pl.BlockSpec((None, None, 2, width), lambda b, k: (b, nt - 1 - k, 0, 0)),
            const(w1.shape), const(gpost.shape),
        ],
        out_specs=pl.BlockSpec((None, tile, width), rev),
        out_shape=jax.ShapeDtypeStruct((bsz, seq, width), F32),
        scratch_shapes=[pltpu.VMEM((1, width), F32)],
        compiler_params=pltpu.CompilerParams(
            dimension_semantics=("arbitrary", "arbitrary"), vmem_limit_bytes=VMEM_LIMIT),
        name="even_fix",
    )(x, yp, pz, summ, w1, gpost)


def _odd_proj_kernel(x_ref, gpre_ref, wq_ref, wk_ref, wv_ref, wr_ref, wlr_ref,
                     q_ref, k_ref, v_ref, r_ref, lr_ref):
    x = x_ref[...]
    h = (x * _rms_scale(x) * gpre_ref[...]).astype(BF16)
    q_ref[...] = _dot(h, wq_ref[...]).astype(BF16)
    k_ref[...] = _dot(h, wk_ref[...]).astype(BF16)
    v_ref[...] = _dot(h, wv_ref[...]).astype(BF16)
    r_ref[...] = _dot(h, wr_ref[...]).astype(BF16)
    lr_ref[...] = _dot(h, wlr_ref[...])


def _odd_proj(x, gpre, wq, wk, wv, wr, wlr):
    bsz, seq, width = x.shape
    tile = ODD_TILE
    nt = seq // tile
    const = lambda shape: pl.BlockSpec(shape, lambda b, t: (0,) * len(shape))
    row = lambda n: pl.BlockSpec((None, tile, n), lambda b, t: (b, t, 0))
    outs = [(wq.shape[1], BF16), (wk.shape[1], BF16), (wv.shape[1], BF16), (wr.shape[1], BF16),
            (wlr.shape[1], F32)]
    return pl.pallas_call(
        _odd_proj_kernel,
        grid=(bsz, nt),
        in_specs=[row(width), const(gpre.shape), const(wq.shape), const(wk.shape), const(wv.shape),
                  const(wr.shape), const(wlr.shape)],
        out_specs=[row(n) for n, _ in outs],
        out_shape=[jax.ShapeDtypeStruct((bsz, seq, n), dt) for n, dt in outs],
        compiler_params=pltpu.CompilerParams(
            dimension_semantics=("arbitrary", "arbitrary"), vmem_limit_bytes=VMEM_LIMIT),
        name="odd_proj",
    )(x, gpre, wq, wk, wv, wr, wlr)


def _gla_kernel(q_ref, k_ref, v_ref, lr_ref, wg_ref, bg_ref, o_ref, state):
    seq, dk = q_ref.shape
    dv = v_ref.shape[1]
    blk = GLA_BLOCK
    nchunk = blk // GLA_CHUNK
    nblk = seq // blk
    scale = dk ** -0.5

    row = lax.broadcasted_iota(jnp.int32, (blk, blk), 0)
    col = lax.broadcasted_iota(jnp.int32, (blk, blk), 1)
    same_chunk = (row // GLA_CHUNK) == (col // GLA_CHUNK)

    for d in range(2):
        reverse = d == 1
        keep = same_chunk & ((col >= row) if reverse else (col <= row))
        tri = jnp.where(keep, 1.0, 0.0).astype(BF16)
        state[...] = jnp.zeros_like(state)

        def block_body(j, _, d=d, reverse=reverse, keep=keep, tri=tri):
            jb = (nblk - 1 - j) if reverse else j
            rows = pl.ds(pl.multiple_of(jb * blk, blk), blk)
            z = _dot(lr_ref[rows, :].astype(BF16), wg_ref[d]) + bg_ref[d]
            g = (jnp.minimum(z, 0.0) - jnp.log1p(jnp.exp(-jnp.abs(z)))) * (1.0 / GLA_NORMALIZER)
            g_hi = g.astype(BF16)
            g_lo = (g - g_hi.astype(F32)).astype(BF16)
            cum2 = _dot(tri, jnp.concatenate([g_hi, g_lo], axis=1))
            bcum = cum2[:, :dk] + cum2[:, dk:]
            bc3 = bcum.reshape(nchunk, GLA_CHUNK, dk)
            edge = 0 if reverse else GLA_CHUNK - 1
            btot3 = bc3[:, edge:edge + 1, :]
            qf = q_ref[rows, :].astype(F32)
            kf = k_ref[rows, :].astype(F32)
            vb = v_ref[rows, :]
            q_in = (qf * jnp.exp(bcum) * scale).astype(BF16)
            k_in = (kf * jnp.exp(-bcum)).astype(BF16)
            k_st = (kf.reshape(bc3.shape) * jnp.exp(btot3 - bc3)).reshape(blk, dk).astype(BF16)
            decay = jnp.exp(btot3)
            scores = lax.dot_general(q_in, k_in, (((1,), (1,)), ((), ())),
                                     preferred_element_type=F32)
            p = jnp.where(keep, scores, 0.0).astype(BF16)
            o_blk = _dot(p, vb)
            order = range(nchunk - 1, -1, -1) if reverse else range(nchunk)
            parts = [None] * nchunk
            st = state[...]
            for n in order:
                cr = slice(n * GLA_CHUNK, (n + 1) * GLA_CHUNK)
                parts[n] = lax.dot_general(q_in[cr, :], st.astype(BF16), (((1,), (1,)), ((), ())),
                                           preferred_element_type=F32)
                upd = lax.dot_general(vb[cr, :], k_st[cr, :], (((0,), (0,)), ((), ())),
                                      preferred_element_type=F32)
                st = st * decay[n] + upd
            state[...] = st
            o_new = o_blk + jnp.concatenate(parts, axis=0)
            if reverse:
                o_ref[rows, :] = o_ref[rows, :] + o_new
            else:
                o_ref[rows, :] = o_new
            return 0

        lax.fori_loop(0, nblk, block_body, 0)


def _gla(q, k, v, lr, wg, bg):
    bsz, seq, key = q.shape
    val = v.shape[2]
    dk = key // GLA_HEADS
    dv = val // GLA_HEADS
    return pl.pallas_call(
        _gla_kernel,
        grid=(bsz, GLA_HEADS),
        in_specs=[
            pl.BlockSpec((None, seq, dk), lambda b, h: (b, 0, h)),
            pl.BlockSpec((None, seq, dk), lambda b, h: (b, 0, h)),
            pl.BlockSpec((None, seq, dv), lambda b, h: (b, 0, h)),
            pl.BlockSpec((None, seq, lr.shape[2]), lambda b, h: (b, 0, 0)),
            pl.BlockSpec((2, wg.shape[1], dk), lambda b, h: (0, 0, h)),
            pl.BlockSpec((2, 1, dk), lambda b, h: (0, 0, h)),
        ],
        out_specs=pl.BlockSpec((None, seq, dv), lambda b, h: (b, 0, h)),
        out_shape=jax.ShapeDtypeStruct((bsz, seq, val), F32),
        scratch_shapes=[pltpu.VMEM((dv, dk), F32)],
        compiler_params=pltpu.CompilerParams(
            dimension_semantics=("arbitrary", "arbitrary"), vmem_limit_bytes=VMEM_LIMIT),
        name="odd_gla",
    )(q, k, v, lr, wg, bg)


def _odd_out_kernel(x_ref, o_ref, r_ref, gh_ref, wout_ref, gpost_ref, out_ref, lhs):
    dv = gh_ref.shape[1]
    gh = gh_ref[...]
    for h in range(GLA_HEADS):
        cs = slice(h * dv, (h + 1) * dv)
        o = o_ref[:, cs]
        gate = _silu(r_ref[:, cs].astype(F32))
        lhs[:, cs] = (o * _rms_scale(o) * gh * gate).astype(BF16)
    y = _dot(lhs[...], wout_ref[...])
    out_ref[...] = x_ref[...] + y * _rms_scale(y) * gpost_ref[...]


def _odd_out(x, o, r, gh, w_out, gpost):
    bsz, seq, width = x.shape
    val = o.shape[2]
    tile = ODD_TILE
    nt = seq // tile
    const = lambda shape: pl.BlockSpec(shape, lambda b, t: (0,) * len(shape))
    row = lambda n: pl.BlockSpec((None, tile, n), lambda b, t: (b, t, 0))
    return pl.pallas_call(
        _odd_out_kernel,
        grid=(bsz, nt),
        in_specs=[row(width), row(val), row(val), const(gh.shape), const(w_out.shape),
                  const(gpost.shape)],
        out_specs=row(width),
        out_shape=jax.ShapeDtypeStruct((bsz, seq, width), F32),
        scratch_shapes=[pltpu.VMEM((tile, val), BF16)],
        compiler_params=pltpu.CompilerParams(
            dimension_semantics=("arbitrary", "arbitrary"), vmem_limit_bytes=VMEM_LIMIT),
        name="odd_out",
    )(x, o, r, gh, w_out, gpost)


def _even_layer(x, norm_pre, norm_post, w_in, conv_w, conv_b, gate_w, gate_b, lam, sc_w, w_out):
    width = x.shape[-1]
    hd = width // RG_HEADS
    gw = jnp.transpose(gate_w, (2, 3, 0, 1, 4)).reshape(RG_HEADS, hd, 4 * hd).astype(BF16)
    gb = gate_b.reshape(4, width)
    w_out_b = w_out.astype(BF16)
    yp, pz, summ = _even_fwd(x, norm_pre[None, :], w_in.astype(BF16), conv_w, conv_b[None, :],
                             gw, gb, lam, sc_w, w_out_b)
    return _even_fix(x, yp, pz, summ, w_out_b[:width], norm_post[None, :])


def _odd_layer(x, norm_pre, norm_post, w_in, w_gate_lr, b_gate, head_norm_g, w_out):
    width = x.shape[-1]
    key = w_gate_lr.shape[-1]
    val = w_out.shape[0]
    w_b = w_in.astype(BF16)
    wq, wk = w_b[:, :key], w_b[:, key:2 * key]
    wv, wr = w_b[:, 2 * key:2 * key + val], w_b[:, 2 * key + val:2 * key + 2 * val]
    wlr = jnp.pad(w_b[:, 2 * key + 2 * val:], ((0, 0), (0, LANES - 2 * GLA_RANK)))
    wg = jnp.zeros((2, LANES, key), F32)
    wg = wg.at[0, :GLA_RANK].set(w_gate_lr[0]).at[1, GLA_RANK:2 * GLA_RANK].set(w_gate_lr[1])
    q, k, v, r, lr = _odd_proj(x, norm_pre[None, :], wq, wk, wv, wr, wlr)
    o = _gla(q, k, v, lr, wg.astype(BF16), b_gate[:, None, :])
    gh = head_norm_g[None, :]
    return _odd_out(x, o, r, gh, w_out.astype(BF16), norm_post[None, :])


def kernel(x, even_norm_pre, even_norm_post, even_w_in, rg_conv_w, rg_conv_b, rg_gate_w, rg_gate_b,
           rg_lambda, sc_conv_w, even_w_out, odd_norm_pre, odd_norm_post, odd_w_in, gla_w_gate_lr,
           gla_b_gate, gla_norm_g, odd_w_out):
    depth = even_norm_pre.shape[0] + odd_norm_pre.shape[0]
    for layer in range(depth):
        j = layer // 2
        if layer % 2 == 0:
            x = _even_layer(x, even_norm_pre[j], even_norm_post[j], even_w_in[j], rg_conv_w[j],
                            rg_conv_b[j], rg_gate_w[j], rg_gate_b[j], rg_lambda[j], sc_conv_w[j],
                            even_w_out[j])
        else:
            x = _odd_layer(x, odd_norm_pre[j], odd_norm_post[j], odd_w_in[j], gla_w_gate_lr[j],
                           gla_b_gate[j], gla_norm_g[j], odd_w_out[j])
    return x
```

```python
import functools

import jax
import jax.numpy as jnp
from jax import lax
from jax.experimental import pallas as pl
from jax.experimental.pallas import tpu as pltpu

F32 = jnp.float32
BF16 = jnp.bfloat16

NORM_EPS = 1e-6
RG_HEADS = 8
RG_C = 8.0
GLA_HEADS = 4
GLA_RANK = 16
GLA_NORMALIZER = 16.0
GLA_CHUNK = 64

SUBLANES = 8
LANES = 128
HALO = 16
EVEN_TILE = 256
ODD_TILE = 512
GLA_BLOCK = 256
GATE_ROWS = 512
VMEM_LIMIT = 56 * 1024 * 1024


def _rms_scale(x):
    return lax.rsqrt(jnp.mean(x * x, axis=-1, keepdims=True) + NORM_EPS)


def _silu(x):
    return x * jax.nn.sigmoid(x)


def _dot(a, b):
    return jnp.dot(a, b, preferred_element_type=F32)


def _slab_scan(a, b, reverse):
    t, c = a.shape
    a3 = a.reshape(t // SUBLANES, SUBLANES, c)
    b3 = b.reshape(t // SUBLANES, SUBLANES, c)
    row = lax.broadcasted_iota(jnp.int32, a3.shape, 1)
    for step in (1, 2, 4):
        if reverse:
            shift = SUBLANES - step
            valid = row < SUBLANES - step
        else:
            shift = step
            valid = row >= step
        a_prev = pltpu.roll(a3, shift, 1)
        b_prev = pltpu.roll(b3, shift, 1)
        am = jnp.where(valid, a3, 0.0)
        b3 = am * b_prev + b3
        a3 = jnp.where(valid, a3 * a_prev, a3)
    return a3.reshape(t, c), b3.reshape(t, c)


def _even_fwd_kernel(xp_ref, x_ref, xn_ref, gpre_ref, win_ref, cw_ref, cb_ref, gw_ref, gb_ref,
                     lam_ref, scw_ref, wout_ref,
                     yp_ref, pz_ref, summ_ref,
                     hbuf, ext_a, ext_b, ua_buf, pf_buf, hf_buf, pb_buf, hb_buf, ycat, carry_f):
    t = pl.program_id(1)
    nt = pl.num_programs(1)
    tile = x_ref.shape[0]
    width = x_ref.shape[1]
    hd = width // RG_HEADS
    nslab = tile // SUBLANES

    gpre = gpre_ref[...]
    for src, lo, n in ((xp_ref, 0, HALO), (x_ref, HALO, tile), (xn_ref, HALO + tile, HALO)):
        xs = src[...]
        hbuf[lo:lo + n, :] = (xs * _rms_scale(xs) * gpre).astype(BF16)

    top_ok = (t > 0).astype(F32)
    bot_ok = (t < nt - 1).astype(F32)

    def halo_mask(ref):
        ref[0:HALO, :] = ref[0:HALO, :] * top_ok
        ref[HALO + tile:, :] = ref[HALO + tile:, :] * bot_ok

    ext_a[...] = _dot(hbuf[...], win_ref[:, 0:width])
    halo_mask(ext_a)
    cw = cw_ref[...]
    ua = cb_ref[...] + cw[0:1, :] * ext_a[HALO - 2:HALO - 2 + tile, :]
    for k in range(1, 4):
        ua = ua + cw[k:k + 1, :] * ext_a[HALO - 2 + k:HALO - 2 + k + tile, :]
    ua_buf[...] = ua

    lam = lam_ref[...]
    log_scale = -RG_C * (jnp.maximum(-lam, 0.0) + jnp.log1p(jnp.exp(-jnp.abs(lam))))
    gb = gb_ref[...]
    for h in range(RG_HEADS):
        cs = slice(h * hd, (h + 1) * hd)
        u = ua_buf[:, cs]
        g = _dot(u.astype(BF16), gw_ref[h])
        for d, (p_buf, h_buf) in enumerate(((pf_buf, hf_buf), (pb_buf, hb_buf))):
            r = jax.nn.sigmoid(g[:, (2 * d) * hd:(2 * d + 1) * hd] + gb[2 * d:2 * d + 1, cs])
            i = jax.nn.sigmoid(g[:, (2 * d + 1) * hd:(2 * d + 2) * hd] + gb[2 * d + 1:2 * d + 2, cs])
            a = jnp.exp(log_scale[d:d + 1, cs] * r)
            b = jnp.sqrt(1.0 - a * a) * (i * u)
            p_loc, h_loc = _slab_scan(a, b, reverse=(d == 1))
            p_buf[:, cs] = p_loc
            h_buf[:, cs] = h_loc

    @pl.when(t == 0)
    def _():
        carry_f[...] = jnp.zeros_like(carry_f)

    def fwd_body(s, c):
        rows = pl.ds(pl.multiple_of(s * SUBLANES, SUBLANES), SUBLANES)
        h = hf_buf[rows, :] + pf_buf[rows, :] * c
        hf_buf[rows, :] = h
        return jnp.broadcast_to(h[SUBLANES - 1:SUBLANES, :], h.shape)

    carry_f[...] = lax.fori_loop(0, nslab, fwd_body, carry_f[...], unroll=4)

    def bwd_body(k, carry):
        c, p = carry
        s = nslab - 1 - k
        rows = pl.ds(pl.multiple_of(s * SUBLANES, SUBLANES), SUBLANES)
        p_loc = pb_buf[rows, :]
        h = hb_buf[rows, :] + p_loc * c
        pp = p_loc * p
        hb_buf[rows, :] = h
        pb_buf[rows, :] = pp
        return (jnp.broadcast_to(h[0:1, :], h.shape), jnp.broadcast_to(pp[0:1, :], pp.shape))

    zeros = jnp.zeros((SUBLANES, width), F32)
    lax.fori_loop(0, nslab, bwd_body, (zeros, zeros + 1.0), unroll=4)
    summ_ref[0:1, :] = hb_buf[0:1, :]
    summ_ref[1:2, :] = pb_buf[0:1, :]

    sza = _silu(_dot(hbuf[HALO:HALO + tile, :], win_ref[:, width:2 * width]))
    ycat[:, 0:width] = ((hf_buf[...] + hb_buf[...]) * sza).astype(BF16)
    pz_ref[...] = (pb_buf[...] * sza).astype(BF16)

    ext_a[...] = _dot(hbuf[...], win_ref[:, 2 * width:3 * width])
    ext_b[...] = _dot(hbuf[...], win_ref[:, 4 * width:5 * width])
    ext_a[...] = ext_a[...] * ext_b[...]
    halo_mask(ext_a)
    scw = scw_ref[...]
    cv = scw[0:1, :] * ext_a[HALO - 1:HALO - 1 + tile, :]
    for k in range(1, 3):
        cv = cv + scw[k:k + 1, :] * ext_a[HALO - 1 + k:HALO - 1 + k + tile, :]
    hcur = hbuf[HALO:HALO + tile, :]
    gate_b = _dot(hcur, win_ref[:, 3 * width:4 * width])
    zb = _dot(hcur, win_ref[:, 5 * width:6 * width])
    ycat[:, width:2 * width] = (gate_b * cv * _silu(zb)).astype(BF16)

    yp_ref[...] = _dot(ycat[...], wout_ref[...])


def _even_fwd(x, gpre, w_in, conv_w, conv_b, gate_w, gate_b, lam, sc_w, w_out):
    bsz, seq, width = x.shape
    tile = EVEN_TILE
    nt = seq // tile
    hpt = tile // HALO
    nh = seq // HALO
    const = lambda shape: pl.BlockSpec(shape, lambda b, t: (0,) * len(shape))
    ext = tile + 2 * HALO
    return pl.pallas_call(
        _even_fwd_kernel,
        grid=(bsz, nt),
        in_specs=[
            pl.BlockSpec((None, HALO, width), lambda b, t: (b, jnp.maximum(t * hpt - 1, 0), 0)),
            pl.BlockSpec((None, tile, width), lambda b, t: (b, t, 0)),
            pl.BlockSpec((None, HALO, width), lambda b, t: (b, jnp.minimum((t + 1) * hpt, nh - 1), 0)),
            const(gpre.shape), const(w_in.shape), const(conv_w.shape), const(conv_b.shape),
            const(gate_w.shape), const(gate_b.shape), const(lam.shape), const(sc_w.shape),
            const(w_out.shape),
        ],
        out_specs=[
            pl.BlockSpec((None, tile, width), lambda b, t: (b, t, 0)),
            pl.BlockSpec((None, tile, width), lambda b, t: (b, t, 0)),
            pl.BlockSpec((None, None, 2, width), lambda b, t: (b, t, 0, 0)),
        ],
        out_shape=[
            jax.ShapeDtypeStruct((bsz, seq, width), F32),
            jax.ShapeDtypeStruct((bsz, seq, width), BF16),
            jax.ShapeDtypeStruct((bsz, nt, 2, width), F32),
        ],
        scratch_shapes=[
            pltpu.VMEM((ext, width), BF16),
            pltpu.VMEM((ext, width), F32),
            pltpu.VMEM((ext, width), F32),
            pltpu.VMEM((tile, width), F32),
            pltpu.VMEM((tile, width), F32),
            pltpu.VMEM((tile, width), F32),
            pltpu.VMEM((tile, width), F32),
            pltpu.VMEM((tile, width), F32),
            pltpu.VMEM((tile, 2 * width), BF16),
            pltpu.VMEM((SUBLANES, width), F32),
        ],
        compiler_params=pltpu.CompilerParams(
            dimension_semantics=("arbitrary", "arbitrary"), vmem_limit_bytes=VMEM_LIMIT),
        name="even_fwd",
    )(x, x, x, gpre, w_in, conv_w, conv_b, gate_w, gate_b, lam, sc_w, w_out)


def _even_fix_kernel(x_ref, yp_ref, pz_ref, summ_ref, w1_ref, gpost_ref, out_ref, carry_b):
    k = pl.program_id(1)

    @pl.when(k == 0)
    def _():
        carry_b[...] = jnp.zeros_like(carry_b)

    c = carry_b[...]
    lhs = (pz_ref[...].astype(F32) * c).astype(BF16)
    y = yp_ref[...] + _dot(lhs, w1_ref[...])
    out_ref[...] = x_ref[...] + y * _rms_scale(y) * gpost_ref[...]
    carry_b[...] = summ_ref[0:1, :] + summ_ref[1:2, :] * c


def _even_fix(x, yp, pz, summ, w1, gpost):
    bsz, seq, width = x.shape
    tile = EVEN_TILE
    nt = seq // tile
    rev = lambda b, k: (b, nt - 1 - k, 0)
    const = lambda shape: pl.BlockSpec(shape, lambda b, k: (0,) * len(shape))
    return pl.pallas_call(
        _even_fix_kernel,
        grid=(bsz, nt),
        in_specs=[
            pl.BlockSpec((None, tile, width), rev),
            pl.BlockSpec((None, tile, width), rev),
            pl.BlockSpec((None, tile, width), rev),
            pl.BlockSpec((None, None, 2, width), lambda b, k: (b, nt - 1 - k, 0, 0)),
            const(w1.shape), const(gpost.shape),
        ],
        out_specs=pl.BlockSpec((None, tile, width), rev),
        out_shape=jax.ShapeDtypeStruct((bsz, seq, width), F32),
        scratch_shapes=[pltpu.VMEM((1, width), F32)],
        compiler_params=pltpu.CompilerParams(
            dimension_semantics=("arbitrary", "arbitrary"), vmem_limit_bytes=VMEM_LIMIT),
        name="even_fix",
    )(x, yp, pz, summ, w1, gpost)


def _odd_proj_kernel(x_ref, gpre_ref, wq_ref, wk_ref, wv_ref, wr_ref, wlr_ref,
                     q_ref, k_ref, v_ref, r_ref, lr_ref):
    x = x_ref[...]
    h = (x * _rms_scale(x) * gpre_ref[...]).astype(BF16)
    q_ref[...] = _dot(h, wq_ref[...]).astype(BF16)
    k_ref[...] = _dot(h, wk_ref[...]).astype(BF16)
    v_ref[...] = _dot(h, wv_ref[...]).astype(BF16)
    r_ref[...] = _dot(h, wr_ref[...]).astype(BF16)
    lr_ref[...] = _dot(h, wlr_ref[...])


def _odd_proj(x, gpre, wq, wk, wv, wr, wlr):
    bsz, seq, width = x.shape
    tile = ODD_TILE
    nt = seq // tile
    const = lambda shape: pl.BlockSpec(shape, lambda b, t: (0,) * len(shape))
    row = lambda n: pl.BlockSpec((None, tile, n), lambda b, t: (b, t, 0))
    outs = [(wq.shape[1], BF16), (wk.shape[1], BF16), (wv.shape[1], BF16), (wr.shape[1], BF16),
            (wlr.shape[1], F32)]
    return pl.pallas_call(
        _odd_proj_kernel,
        grid=(bsz, nt),
        in_specs=[row(width), const(gpre.shape), const(wq.shape), const(wk.shape), const(wv.shape),
                  const(wr.shape), const(wlr.shape)],
        out_specs=[row(n) for n, _ in outs],
        out_shape=[jax.ShapeDtypeStruct((bsz, seq, n), dt) for n, dt in outs],
        compiler_params=pltpu.CompilerParams(
            dimension_semantics=("arbitrary", "arbitrary"), vmem_limit_bytes=VMEM_LIMIT),
        name="odd_proj",
    )(x, gpre, wq, wk, wv, wr, wlr)


def _gla_kernel(q_ref, k_ref, v_ref, lr_ref, wg_ref, bg_ref, o_ref, g2, tri, keep, state):
    seq, dk = q_ref.shape
    dv = v_ref.shape[1]
    blk = GLA_BLOCK
    nchunk = blk // GLA_CHUNK
    nblk = seq // blk
    scale = dk ** -0.5

    row = lax.broadcasted_iota(jnp.int32, (blk, blk), 0)
    col = lax.broadcasted_iota(jnp.int32, (blk, blk), 1)
    same_chunk = (row // GLA_CHUNK) == (col // GLA_CHUNK)
    for d in range(2):
        m = jnp.where(same_chunk & ((col >= row) if d else (col <= row)), 1.0, 0.0)
        keep[d] = m
        tri[d] = m.astype(BF16)
    state[...] = jnp.zeros_like(state)

    def gate_body(i, _):
        rows = pl.ds(pl.multiple_of(i * GATE_ROWS, GATE_ROWS), GATE_ROWS)
        z = _dot(lr_ref[rows, :].astype(BF16), wg_ref[...]) + bg_ref[...]
        g = (jnp.minimum(z, 0.0) - jnp.log(1.0 + jnp.exp(-jnp.abs(z)))) * (1.0 / GLA_NORMALIZER)
        hi = g.astype(BF16)
        lo = (g - hi.astype(F32)).astype(BF16)
        for d in range(2):
            g2[rows, (2 * d) * dk:(2 * d + 1) * dk] = hi[:, d * dk:(d + 1) * dk]
            g2[rows, (2 * d + 1) * dk:(2 * d + 2) * dk] = lo[:, d * dk:(d + 1) * dk]
        return 0

    lax.fori_loop(0, seq // GATE_ROWS, gate_body, 0)

    nt_dims = (((1,), (1,)), ((), ()))
    tn_dims = (((0,), (0,)), ((), ()))
    dirs = (0, 1)

    def pair_body(accumulate, j, _):
        rows = [pl.ds(pl.multiple_of(jb * blk, blk), blk) for jb in (j, nblk - 1 - j)]
        cum2 = [_dot(tri[d], g2[rows[d], (2 * d) * dk:(2 * d + 2) * dk]) for d in dirs]
        bc3 = [(c[:, :dk] + c[:, dk:]).reshape(nchunk, GLA_CHUNK, dk) for c in cum2]
        btot3 = [bc3[0][:, GLA_CHUNK - 1:GLA_CHUNK, :], bc3[1][:, 0:1, :]]
        vb = [v_ref[rows[d], :] for d in dirs]
        kst_f = [k_ref[rows[d], :].astype(F32).reshape(bc3[d].shape) * jnp.exp(btot3[d] - bc3[d])
                 for d in dirs]
        k_st = [x.reshape(blk, dk).astype(BF16) for x in kst_f]
        upd = [[lax.dot_general(vb[d][n * GLA_CHUNK:(n + 1) * GLA_CHUNK, :],
                                k_st[d][n * GLA_CHUNK:(n + 1) * GLA_CHUNK, :], tn_dims,
                                preferred_element_type=F32) for n in range(nchunk)]
               for d in dirs]
        k_in = [(kst_f[d] * jnp.exp(-btot3[d])).reshape(blk, dk).astype(BF16) for d in dirs]
        q_in = [(q_ref[rows[d], :].astype(F32) * jnp.exp(bc3[d].reshape(blk, dk)) * scale).astype(BF16)
                for d in dirs]
        scores = [lax.dot_general(q_in[d], k_in[d], nt_dims, preferred_element_type=F32)
                  for d in dirs]
        decay = [jnp.exp(b) for b in btot3]
        entry = []
        for d in dirs:
            order = range(nchunk - 1, -1, -1) if d else range(nchunk)
            st = state[d]
            ent = [None] * nchunk
            for n in order:
                ent[n] = st.astype(BF16)
                st = st * decay[d][n] + upd[d][n]
            state[d] = st
            entry.append(ent)
        p = [jnp.where(keep[d] > 0.0, scores[d], 0.0).astype(BF16) for d in dirs]
        o_blk = [_dot(p[d], vb[d]) for d in dirs]
        for d in dirs:
            parts = [lax.dot_general(q_in[d][n * GLA_CHUNK:(n + 1) * GLA_CHUNK, :], entry[d][n],
                                     nt_dims, preferred_element_type=F32) for n in range(nchunk)]
            o_new = o_blk[d] + jnp.concatenate(parts, axis=0)
            if accumulate:
                o_ref[rows[d], :] = o_ref[rows[d], :] + o_new
            else:
                o_ref[rows[d], :] = o_new
        return 0

    lax.fori_loop(0, nblk // 2, functools.partial(pair_body, False), 0)
    lax.fori_loop(nblk // 2, nblk, functools.partial(pair_body, True), 0)


def _gla(q, k, v, lr, wg, bg):
    bsz, seq, key = q.shape
    val = v.shape[2]
    dk = key // GLA_HEADS
    dv = val // GLA_HEADS
    return pl.pallas_call(
        _gla_kernel,
        grid=(bsz, GLA_HEADS),
        in_specs=[
            pl.BlockSpec((None, seq, dk), lambda b, h: (b, 0, h)),
            pl.BlockSpec((None, seq, dk), lambda b, h: (b, 0, h)),
            pl.BlockSpec((None, seq, dv), lambda b, h: (b, 0, h)),
            pl.BlockSpec((None, seq, lr.shape[2]), lambda b, h: (b, 0, 0)),
            pl.BlockSpec((None, wg.shape[1], 2 * dk), lambda b, h: (h, 0, 0)),
            pl.BlockSpec((None, 1, 2 * dk), lambda b, h: (h, 0, 0)),
        ],
        out_specs=pl.BlockSpec((None, seq, dv), lambda b, h: (b, 0, h)),
        out_shape=jax.ShapeDtypeStruct((bsz, seq, val), F32),
        scratch_shapes=[
            pltpu.VMEM((seq, 4 * dk), BF16),
            pltpu.VMEM((2, GLA_BLOCK, GLA_BLOCK), BF16),
            pltpu.VMEM((2, GLA_BLOCK, GLA_BLOCK), F32),
            pltpu.VMEM((2, dv, dk), F32),
        ],
        compiler_params=pltpu.CompilerParams(
            dimension_semantics=("arbitrary", "arbitrary"), vmem_limit_bytes=VMEM_LIMIT),
        name="odd_gla",
    )(q, k, v, lr, wg, bg)


def _odd_out_kernel(x_ref, o_ref, r_ref, gh_ref, wout_ref, gpost_ref, out_ref, lhs):
    dv = gh_ref.shape[1]
    gh = gh_ref[...]
    for h in range(GLA_HEADS):
        cs = slice(h * dv, (h + 1) * dv)
        o = o_ref[:, cs]
        gate = _silu(r_ref[:, cs].astype(F32))
        lhs[:, cs] = (o * _rms_scale(o) * gh * gate).astype(BF16)
    y = _dot(lhs[...], wout_ref[...])
    out_ref[...] = x_ref[...] + y * _rms_scale(y) * gpost_ref[...]


def _odd_out(x, o, r, gh, w_out, gpost):
    bsz, seq, width = x.shape
    val = o.shape[2]
    tile = ODD_TILE
    nt = seq // tile
    const = lambda shape: pl.BlockSpec(shape, lambda b, t: (0,) * len(shape))
    row = lambda n: pl.BlockSpec((None, tile, n), lambda b, t: (b, t, 0))
    return pl.pallas_call(
        _odd_out_kernel,
        grid=(bsz, nt),
        in_specs=[row(width), row(val), row(val), const(gh.shape), const(w_out.shape),
                  const(gpost.shape)],
        out_specs=row(width),
        out_shape=jax.ShapeDtypeStruct((bsz, seq, width), F32),
        scratch_shapes=[pltpu.VMEM((tile, val), BF16)],
        compiler_params=pltpu.CompilerParams(
            dimension_semantics=("arbitrary", "arbitrary"), vmem_limit_bytes=VMEM_LIMIT),
        name="odd_out",
    )(x, o, r, gh, w_out, gpost)


def _even_layer(x, norm_pre, norm_post, w_in, conv_w, conv_b, gate_w, gate_b, lam, sc_w, w_out):
    width = x.shape[-1]
    hd = width // RG_HEADS
    gw = jnp.transpose(gate_w, (2, 3, 0, 1, 4)).reshape(RG_HEADS, hd, 4 * hd).astype(BF16)
    gb = gate_b.reshape(4, width)
    w_out_b = w_out.astype(BF16)
    yp, pz, summ = _even_fwd(x, norm_pre[None, :], w_in.astype(BF16), conv_w, conv_b[None, :],
                             gw, gb, lam, sc_w, w_out_b)
    return _even_fix(x, yp, pz, summ, w_out_b[:width], norm_post[None, :])


def _odd_layer(x, norm_pre, norm_post, w_in, w_gate_lr, b_gate, head_norm_g, w_out):
    width = x.shape[-1]
    key = w_gate_lr.shape[-1]
    val = w_out.shape[0]
    w_b = w_in.astype(BF16)
    wq, wk = w_b[:, :key], w_b[:, key:2 * key]
    wv, wr = w_b[:, 2 * key:2 * key + val], w_b[:, 2 * key + val:2 * key + 2 * val]
    wlr = jnp.pad(w_b[:, 2 * key + 2 * val:], ((0, 0), (0, LANES - 2 * GLA_RANK)))
    wg = jnp.zeros((2, LANES, key), F32)
    wg = wg.at[0, :GLA_RANK].set(w_gate_lr[0]).at[1, GLA_RANK:2 * GLA_RANK].set(w_gate_lr[1])
    q, k, v, r, lr = _odd_proj(x, norm_pre[None, :], wq, wk, wv, wr, wlr)
    dk = key // GLA_HEADS
    wg_h = jnp.transpose(wg.reshape(2, LANES, GLA_HEADS, dk), (2, 1, 0, 3))
    wg_h = wg_h.reshape(GLA_HEADS, LANES, 2 * dk).astype(BF16)
    bg_h = jnp.transpose(b_gate.reshape(2, GLA_HEADS, dk), (1, 0, 2)).reshape(GLA_HEADS, 1, 2 * dk)
    o = _gla(q, k, v, lr, wg_h, bg_h)
    gh = head_norm_g[None, :]
    return _odd_out(x, o, r, gh, w_out.astype(BF16), norm_post[None, :])


def kernel(x, even_norm_pre, even_norm_post, even_w_in, rg_conv_w, rg_conv_b, rg_gate_w, rg_gate_b,
           rg_lambda, sc_conv_w, even_w_out, odd_norm_pre, odd_norm_post, odd_w_in, gla_w_gate_lr,
           gla_b_gate, gla_norm_g, odd_w_out):
    depth = even_norm_pre.shape[0] + odd_norm_pre.shape[0]
    for layer in range(depth):
        j = layer // 2
        if layer % 2 == 0:
            x = _even_layer(x, even_norm_pre[j], even_norm_post[j], even_w_in[j], rg_conv_w[j],
                            rg_conv_b[j], rg_gate_w[j], rg_gate_b[j], rg_lambda[j], sc_conv_w[j],
                            even_w_out[j])
        else:
            x = _odd_layer(x, odd_norm_pre[j], odd_norm_post[j], odd_w_in[j], gla_w_gate_lr[j],
                           gla_b_gate[j], gla_norm_g[j], odd_w_out[j])
    return x
```

```python
import functools

import numpy as np
import jax
import jax.numpy as jnp
from jax import lax
from jax.experimental import pallas as pl
from jax.experimental.pallas import tpu as pltpu

F32 = jnp.float32
BF16 = jnp.bfloat16

NORM_EPS = 1e-6
RG_HEADS = 8
RG_C = 8.0
GLA_HEADS = 4
GLA_RANK = 16
GLA_NORMALIZER = 16.0
GLA_CHUNK = 64

SUBLANES = 8
LANES = 128
EVEN_STEPS = 64
PERM_STEPS = 32
HALO_STEPS = 2
HALO_FETCH = 8
HALO = HALO_STEPS * SUBLANES
ODD_TILE = 512
GLA_BLOCK = 256
GATE_ROWS = 512
VMEM_LIMIT = 56 * 1024 * 1024


def _rms_scale(x):
    return lax.rsqrt(jnp.mean(x * x, axis=-1, keepdims=True) + NORM_EPS)


def _silu(x):
    return x * jax.nn.sigmoid(x)


def _dot(a, b):
    return jnp.dot(a, b, preferred_element_type=F32)


def _const_spec(shape):
    return pl.BlockSpec(shape, lambda *_: (0,) * len(shape), pipeline_mode=pl.Buffered(1))


def _time_major_perm(steps, first, count):
    p = np.zeros((count * SUBLANES, steps * SUBLANES), np.float32)
    for j in range(count):
        for b in range(SUBLANES):
            p[j * SUBLANES + b, b * steps + first + j] = 1.0
    return p


def _even_fwd_kernel(xp_ref, x_ref, xn_ref, perm_ref, permp_ref, permn_ref, gpre_ref, win_ref,
                     cw_ref, cb_ref, gw_ref, gb_ref, lam_ref, scw_ref, wout_ref,
                     yp_ref, pz_ref, summ_ref,
                     hbuf, ext_a, ext_b, ua_buf, af_buf, hf_buf, pb_buf, hb_buf, ycat, carry_f):
    t = pl.program_id(0)
    nt = pl.num_programs(0)
    nb, steps, width = x_ref.shape
    rows = nb * steps
    prows = nb * PERM_STEPS
    hd = width // RG_HEADS

    gpre = gpre_ref[...]

    def normed(xs):
        return (xs * _rms_scale(xs) * gpre).astype(BF16)

    hp = normed(xp_ref[...].reshape(nb * HALO_FETCH, width))
    hbuf[0:HALO, :] = _dot(permp_ref[...], hp).astype(BF16)
    hn = normed(xn_ref[...].reshape(nb * HALO_FETCH, width))
    hbuf[HALO + rows:, :] = _dot(permn_ref[...], hn).astype(BF16)
    for s in range(steps // PERM_STEPS):
        xs = x_ref[:, s * PERM_STEPS:(s + 1) * PERM_STEPS, :].reshape(prows, width)
        hbuf[HALO + s * prows:HALO + (s + 1) * prows, :] = _dot(perm_ref[...], normed(xs)).astype(BF16)

    top_ok = (t > 0).astype(F32)
    bot_ok = (t < nt - 1).astype(F32)

    def halo_mask(ref):
        ref[0:HALO, :] = ref[0:HALO, :] * top_ok
        ref[HALO + rows:, :] = ref[HALO + rows:, :] * bot_ok

    def tap(ref, shift):
        lo = HALO + shift * SUBLANES
        return ref[lo:lo + rows, :]

    ext_a[...] = _dot(hbuf[...], win_ref[:, 0:width])
    halo_mask(ext_a)
    cw = cw_ref[...]
    ua = cb_ref[...] + cw[0:1, :] * tap(ext_a, -2)
    for k in range(1, 4):
        ua = ua + cw[k:k + 1, :] * tap(ext_a, k - 2)
    ua_buf[...] = ua

    lam = lam_ref[...]
    log_scale = -RG_C * (jnp.maximum(-lam, 0.0) + jnp.log1p(jnp.exp(-jnp.abs(lam))))
    gb = gb_ref[...]
    for h in range(RG_HEADS):
        cs = slice(h * hd, (h + 1) * hd)
        u = ua_buf[:, cs]
        g = _dot(u.astype(BF16), gw_ref[h])
        for d, (a_buf, b_buf) in enumerate(((af_buf, hf_buf), (pb_buf, hb_buf))):
            r = jax.nn.sigmoid(g[:, (2 * d) * hd:(2 * d + 1) * hd] + gb[2 * d:2 * d + 1, cs])
            i = jax.nn.sigmoid(g[:, (2 * d + 1) * hd:(2 * d + 2) * hd] + gb[2 * d + 1:2 * d + 2, cs])
            a = jnp.exp(log_scale[d:d + 1, cs] * r)
            a_buf[:, cs] = a
            b_buf[:, cs] = jnp.sqrt(1.0 - a * a) * (i * u)

    @pl.when(t == 0)
    def _():
        carry_f[...] = jnp.zeros_like(carry_f)

    def scan_body(k, carry):
        hf, hb, pb = carry
        rf = pl.ds(pl.multiple_of(k * SUBLANES, SUBLANES), SUBLANES)
        rb = pl.ds(pl.multiple_of((steps - 1 - k) * SUBLANES, SUBLANES), SUBLANES)
        hf = af_buf[rf, :] * hf + hf_buf[rf, :]
        hf_buf[rf, :] = hf
        a = pb_buf[rb, :]
        hb = a * hb + hb_buf[rb, :]
        pb = a * pb
        hb_buf[rb, :] = hb
        pb_buf[rb, :] = pb
        return hf, hb, pb

    zeros = jnp.zeros((SUBLANES, width), F32)
    hf, hb, pb = lax.fori_loop(0, steps, scan_body, (carry_f[...], zeros, zeros + 1.0), unroll=4)
    carry_f[...] = hf
    summ_ref[0] = hb
    summ_ref[1] = pb

    sza = _silu(_dot(hbuf[HALO:HALO + rows, :], win_ref[:, width:2 * width]))
    ycat[:, 0:width] = ((hf_buf[...] + hb_buf[...]) * sza).astype(BF16)
    pz_ref[...] = (pb_buf[...] * sza).astype(BF16)

    ext_a[...] = _dot(hbuf[...], win_ref[:, 2 * width:3 * width])
    ext_b[...] = _dot(hbuf[...], win_ref[:, 4 * width:5 * width])
    ext_a[...] = ext_a[...] * ext_b[...]
    halo_mask(ext_a)
    scw = scw_ref[...]
    cv = scw[0:1, :] * tap(ext_a, -1)
    for k in range(1, 3):
        cv = cv + scw[k:k + 1, :] * tap(ext_a, k - 1)
    hcur = hbuf[HALO:HALO + rows, :]
    gate_b = _dot(hcur, win_ref[:, 3 * width:4 * width])
    zb = _dot(hcur, win_ref[:, 5 * width:6 * width])
    ycat[:, width:2 * width] = (gate_b * cv * _silu(zb)).astype(BF16)

    yp_ref[...] = _dot(ycat[...], wout_ref[...])


def _even_fwd(x, gpre, w_in, conv_w, conv_b, gate_w, gate_b, lam, sc_w, w_out):
    nb, seq, width = x.shape
    steps = EVEN_STEPS
    rows = nb * steps
    nt = seq // steps
    fpt = steps // HALO_FETCH
    nf = seq // HALO_FETCH
    ext = rows + 2 * HALO
    perm = jnp.asarray(_time_major_perm(PERM_STEPS, 0, PERM_STEPS), BF16)
    perm_prev = jnp.asarray(_time_major_perm(HALO_FETCH, HALO_FETCH - HALO_STEPS, HALO_STEPS), BF16)
    perm_next = jnp.asarray(_time_major_perm(HALO_FETCH, 0, HALO_STEPS), BF16)
    consts = (perm, perm_prev, perm_next, gpre, w_in, conv_w, conv_b, gate_w, gate_b, lam, sc_w, w_out)
    return pl.pallas_call(
        _even_fwd_kernel,
        grid=(nt,),
        in_specs=[
            pl.BlockSpec((nb, HALO_FETCH, width), lambda t: (0, jnp.maximum(t * fpt - 1, 0), 0)),
            pl.BlockSpec((nb, steps, width), lambda t: (0, t, 0)),
            pl.BlockSpec((nb, HALO_FETCH, width), lambda t: (0, jnp.minimum((t + 1) * fpt, nf - 1), 0)),
        ] + [_const_spec(c.shape) for c in consts],
        out_specs=[
            pl.BlockSpec((rows, width), lambda t: (t, 0)),
            pl.BlockSpec((rows, width), lambda t: (t, 0)),
            pl.BlockSpec((None, 2, nb, width), lambda t: (t, 0, 0, 0)),
        ],
        out_shape=[
            jax.ShapeDtypeStruct((seq * nb, width), F32),
            jax.ShapeDtypeStruct((seq * nb, width), BF16),
            jax.ShapeDtypeStruct((nt, 2, nb, width), F32),
        ],
        scratch_shapes=[
            pltpu.VMEM((ext, width), BF16),
            pltpu.VMEM((ext, width), F32),
            pltpu.VMEM((ext, width), F32),
            pltpu.VMEM((rows, width), F32),
            pltpu.VMEM((rows, width), F32),
            pltpu.VMEM((rows, width), F32),
            pltpu.VMEM((rows, width), F32),
            pltpu.VMEM((rows, width), F32),
            pltpu.VMEM((rows, 2 * width), BF16),
            pltpu.VMEM((nb, width), F32),
        ],
        compiler_params=pltpu.CompilerParams(
            dimension_semantics=("arbitrary",), vmem_limit_bytes=VMEM_LIMIT),
        name="even_fwd",
    )(x, x, x, *consts)


def _even_fix_kernel(x_ref, yp_ref, pz_ref, summ_ref, permt_ref, w1_ref, gpost_ref, out_ref, carry_b):
    k = pl.program_id(0)
    nb, steps, width = x_ref.shape
    prows = nb * PERM_STEPS

    @pl.when(k == 0)
    def _():
        carry_b[...] = jnp.zeros_like(carry_b)

    c = carry_b[...]
    pz = pz_ref[...].astype(F32).reshape(steps, nb, width)
    lhs = (pz * c[None, :, :]).reshape(steps * nb, width).astype(BF16)
    y = yp_ref[...] + _dot(lhs, w1_ref[...])
    z = y * _rms_scale(y) * gpost_ref[...]
    for s in range(steps // PERM_STEPS):
        zs = z[s * prows:(s + 1) * prows, :]
        hi = zs.astype(BF16)
        lo = (zs - hi.astype(F32)).astype(BF16)
        nat = _dot(permt_ref[...], hi) + _dot(permt_ref[...], lo)
        sl = slice(s * PERM_STEPS, (s + 1) * PERM_STEPS)
        out_ref[:, sl, :] = x_ref[:, sl, :] + nat.reshape(nb, PERM_STEPS, width)
    carry_b[...] = summ_ref[0] + summ_ref[1] * c


def _even_fix(x, yp, pz, summ, w1, gpost):
    nb, seq, width = x.shape
    steps = EVEN_STEPS
    rows = nb * steps
    nt = seq // steps
    perm_t = jnp.asarray(_time_major_perm(PERM_STEPS, 0, PERM_STEPS).T, BF16)
    consts = (perm_t, w1, gpost)
    return pl.pallas_call(
        _even_fix_kernel,
        grid=(nt,),
        in_specs=[
            pl.BlockSpec((nb, steps, width), lambda k: (0, nt - 1 - k, 0)),
            pl.BlockSpec((rows, width), lambda k: (nt - 1 - k, 0)),
            pl.BlockSpec((rows, width), lambda k: (nt - 1 - k, 0)),
            pl.BlockSpec((None, 2, nb, width), lambda k: (nt - 1 - k, 0, 0, 0)),
        ] + [_const_spec(c.shape) for c in consts],
        out_specs=pl.BlockSpec((nb, steps, width), lambda k: (0, nt - 1 - k, 0)),
        out_shape=jax.ShapeDtypeStruct((nb, seq, width), F32),
        scratch_shapes=[pltpu.VMEM((nb, width), F32)],
        compiler_params=pltpu.CompilerParams(
            dimension_semantics=("arbitrary",), vmem_limit_bytes=VMEM_LIMIT),
        name="even_fix",
    )(x, yp, pz, summ, *consts)


def _odd_proj_kernel(x_ref, gpre_ref, wq_ref, wk_ref, wv_ref, wr_ref, wlr_ref,
                     q_ref, k_ref, v_ref, r_ref, lr_ref):
    x = x_ref[...]
    h = (x * _rms_scale(x) * gpre_ref[...]).astype(BF16)
    q_ref[...] = _dot(h, wq_ref[...]).astype(BF16)
    k_ref[...] = _dot(h, wk_ref[...]).astype(BF16)
    v_ref[...] = _dot(h, wv_ref[...]).astype(BF16)
    r_ref[...] = _dot(h, wr_ref[...]).astype(BF16)
    lr_ref[...] = _dot(h, wlr_ref[...])


def _odd_proj(x, gpre, wq, wk, wv, wr, wlr):
    bsz, seq, width = x.shape
    tile = ODD_TILE
    nt = seq // tile
    row = lambda n: pl.BlockSpec((None, tile, n), lambda b, t: (b, t, 0))
    outs = [(wq.shape[1], BF16), (wk.shape[1], BF16), (wv.shape[1], BF16), (wr.shape[1], BF16),
            (wlr.shape[1], F32)]
    consts = (gpre, wq, wk, wv, wr, wlr)
    return pl.pallas_call(
        _odd_proj_kernel,
        grid=(bsz, nt),
        in_specs=[row(width)] + [_const_spec(c.shape) for c in consts],
        out_specs=[row(n) for n, _ in outs],
        out_shape=[jax.ShapeDtypeStruct((bsz, seq, n), dt) for n, dt in outs],
        compiler_params=pltpu.CompilerParams(
            dimension_semantics=("arbitrary", "arbitrary"), vmem_limit_bytes=VMEM_LIMIT),
        name="odd_proj",
    )(x, *consts)


def _gla_kernel(q_ref, k_ref, v_ref, lr_ref, wg_ref, bg_ref, o_ref, g2, tri, keep, state):
    seq, dk = q_ref.shape
    dv = v_ref.shape[1]
    blk = GLA_BLOCK
    nchunk = blk // GLA_CHUNK
    nblk = seq // blk
    scale = dk ** -0.5

    row = lax.broadcasted_iota(jnp.int32, (blk, blk), 0)
    col = lax.broadcasted_iota(jnp.int32, (blk, blk), 1)
    same_chunk = (row // GLA_CHUNK) == (col // GLA_CHUNK)
    for d in range(2):
        m = jnp.where(same_chunk & ((col >= row) if d else (col <= row)), 1.0, 0.0)
        keep[d] = m
        tri[d] = m.astype(BF16)
    state[...] = jnp.zeros_like(state)

    def gate_body(i, _):
        rows = pl.ds(pl.multiple_of(i * GATE_ROWS, GATE_ROWS), GATE_ROWS)
        z = _dot(lr_ref[rows, :].astype(BF16), wg_ref[...]) + bg_ref[...]
        g = (jnp.minimum(z, 0.0) - jnp.log(1.0 + jnp.exp(-jnp.abs(z)))) * (1.0 / GLA_NORMALIZER)
        hi = g.astype(BF16)
        lo = (g - hi.astype(F32)).astype(BF16)
        for d in range(2):
            g2[rows, (2 * d) * dk:(2 * d + 1) * dk] = hi[:, d * dk:(d + 1) * dk]
            g2[rows, (2 * d + 1) * dk:(2 * d + 2) * dk] = lo[:, d * dk:(d + 1) * dk]
        return 0

    lax.fori_loop(0, seq // GATE_ROWS, gate_body, 0)

    nt_dims = (((1,), (1,)), ((), ()))
    tn_dims = (((0,), (0,)), ((), ()))
    dirs = (0, 1)

    def pair_body(accumulate, j, _):
        rows = [pl.ds(pl.multiple_of(jb * blk, blk), blk) for jb in (j, nblk - 1 - j)]
        cum2 = [_dot(tri[d], g2[rows[d], (2 * d) * dk:(2 * d + 2) * dk]) for d in dirs]
        bc3 = [(c[:, :dk] + c[:, dk:]).reshape(nchunk, GLA_CHUNK, dk) for c in cum2]
        btot3 = [bc3[0][:, GLA_CHUNK - 1:GLA_CHUNK, :], bc3[1][:, 0:1, :]]
        vb = [v_ref[rows[d], :] for d in dirs]
        kst_f = [k_ref[rows[d], :].astype(F32).reshape(bc3[d].shape) * jnp.exp(btot3[d] - bc3[d])
                 for d in dirs]
        k_st = [x.reshape(blk, dk).astype(BF16) for x in kst_f]
        upd = [[lax.dot_general(vb[d][n * GLA_CHUNK:(n + 1) * GLA_CHUNK, :],
                                k_st[d][n * GLA_CHUNK:(n + 1) * GLA_CHUNK, :], tn_dims,
                                preferred_element_type=F32) for n in range(nchunk)]
               for d in dirs]
        k_in = [(kst_f[d] * jnp.exp(-btot3[d])).reshape(blk, dk).astype(BF16) for d in dirs]
        q_in = [(q_ref[rows[d], :].astype(F32) * jnp.exp(bc3[d].reshape(blk, dk)) * scale).astype(BF16)
                for d in dirs]
        scores = [lax.dot_general(q_in[d], k_in[d], nt_dims, preferred_element_type=F32)
                  for d in dirs]
        decay = [jnp.exp(b) for b in btot3]
        entry = []
        for d in dirs:
            order = range(nchunk - 1, -1, -1) if d else range(nchunk)
            st = state[d]
            ent = [None] * nchunk
            for n in order:
                ent[n] = st.astype(BF16)
                st = st * decay[d][n] + upd[d][n]
            state[d] = st
            entry.append(ent)
        p = [jnp.where(keep[d] > 0.0, scores[d], 0.0).astype(BF16) for d in dirs]
        o_blk = [_dot(p[d], vb[d]) for d in dirs]
        for d in dirs:
            parts = [lax.dot_general(q_in[d][n * GLA_CHUNK:(n + 1) * GLA_CHUNK, :], entry[d][n],
                                     nt_dims, preferred_element_type=F32) for n in range(nchunk)]
            o_new = o_blk[d] + jnp.concatenate(parts, axis=0)
            if accumulate:
                o_ref[rows[d], :] = o_ref[rows[d], :] + o_new
            else:
                o_ref[rows[d], :] = o_new
        return 0

    lax.fori_loop(0, nblk // 2, functools.partial(pair_body, False), 0)
    lax.fori_loop(nblk // 2, nblk, functools.partial(pair_body, True), 0)


def _gla(q, k, v, lr, wg, bg):
    bsz, seq, key = q.shape
    val = v.shape[2]
    dk = key // GLA_HEADS
    dv = val // GLA_HEADS
    return pl.pallas_call(
        _gla_kernel,
        grid=(bsz, GLA_HEADS),
        in_specs=[
            pl.BlockSpec((None, seq, dk), lambda b, h: (b, 0, h)),
            pl.BlockSpec((None, seq, dk), lambda b, h: (b, 0, h)),
            pl.BlockSpec((None, seq, dv), lambda b, h: (b, 0, h)),
            pl.BlockSpec((None, seq, lr.shape[2]), lambda b, h: (b, 0, 0)),
            pl.BlockSpec((None, wg.shape[1], 2 * dk), lambda b, h: (h, 0, 0)),
            pl.BlockSpec((None, 1, 2 * dk), lambda b, h: (h, 0, 0)),
        ],
        out_specs=pl.BlockSpec((None, seq, dv), lambda b, h: (b, 0, h)),
        out_shape=jax.ShapeDtypeStruct((bsz, seq, val), F32),
        scratch_shapes=[
            pltpu.VMEM((seq, 4 * dk), BF16),
            pltpu.VMEM((2, GLA_BLOCK, GLA_BLOCK), BF16),
            pltpu.VMEM((2, GLA_BLOCK, GLA_BLOCK), F32),
            pltpu.VMEM((2, dv, dk), F32),
        ],
        compiler_params=pltpu.CompilerParams(
            dimension_semantics=("arbitrary", "arbitrary"), vmem_limit_bytes=VMEM_LIMIT),
        name="odd_gla",
    )(q, k, v, lr, wg, bg)


def _odd_out_kernel(x_ref, o_ref, r_ref, gh_ref, wout_ref, gpost_ref, out_ref, lhs):
    dv = gh_ref.shape[1]
    gh = gh_ref[...]
    for h in range(GLA_HEADS):
        cs = slice(h * dv, (h + 1) * dv)
        o = o_ref[:, cs]
        gate = _silu(r_ref[:, cs].astype(F32))
        lhs[:, cs] = (o * _rms_scale(o) * gh * gate).astype(BF16)
    y = _dot(lhs[...], wout_ref[...])
    out_ref[...] = x_ref[...] + y * _rms_scale(y) * gpost_ref[...]


def _odd_out(x, o, r, gh, w_out, gpost):
    bsz, seq, width = x.shape
    val = o.shape[2]
    tile = ODD_TILE
    nt = seq // tile
    row = lambda n: pl.BlockSpec((None, tile, n), lambda b, t: (b, t, 0))
    consts = (gh, w_out, gpost)
    return pl.pallas_call(
        _odd_out_kernel,
        grid=(bsz, nt),
        in_specs=[row(width), row(val), row(val)] + [_const_spec(c.shape) for c in consts],
        out_specs=row(width),
        out_shape=jax.ShapeDtypeStruct((bsz, seq, width), F32),
        scratch_shapes=[pltpu.VMEM((tile, val), BF16)],
        compiler_params=pltpu.CompilerParams(
            dimension_semantics=("arbitrary", "arbitrary"), vmem_limit_bytes=VMEM_LIMIT),
        name="odd_out",
    )(x, o, r, *consts)


def _even_layer(x, norm_pre, norm_post, w_in, conv_w, conv_b, gate_w, gate_b, lam, sc_w, w_out):
    bsz, _, width = x.shape
    assert bsz == SUBLANES, "the time-major even layer needs batch == 8"
    hd = width // RG_HEADS
    gw = jnp.transpose(gate_w, (2, 3, 0, 1, 4)).reshape(RG_HEADS, hd, 4 * hd).astype(BF16)
    gb = gate_b.reshape(4, width)
    w_out_b = w_out.astype(BF16)
    yp, pz, summ = _even_fwd(x, norm_pre[None, :], w_in.astype(BF16), conv_w, conv_b[None, :],
                             gw, gb, lam, sc_w, w_out_b)
    return _even_fix(x, yp, pz, summ, w_out_b[:width], norm_post[None, :])


def _odd_layer(x, norm_pre, norm_post, w_in, w_gate_lr, b_gate, head_norm_g, w_out):
    key = w_gate_lr.shape[-1]
    val = w_out.shape[0]
    w_b = w_in.astype(BF16)
    wq, wk = w_b[:, :key], w_b[:, key:2 * key]
    wv, wr = w_b[:, 2 * key:2 * key + val], w_b[:, 2 * key + val:2 * key + 2 * val]
    wlr = jnp.pad(w_b[:, 2 * key + 2 * val:], ((0, 0), (0, LANES - 2 * GLA_RANK)))
    wg = jnp.zeros((2, LANES, key), F32)
    wg = wg.at[0, :GLA_RANK].set(w_gate_lr[0]).at[1, GLA_RANK:2 * GLA_RANK].set(w_gate_lr[1])
    q, k, v, r, lr = _odd_proj(x, norm_pre[None, :], wq, wk, wv, wr, wlr)
    dk = key // GLA_HEADS
    wg_h = jnp.transpose(wg.reshape(2, LANES, GLA_HEADS, dk), (2, 1, 0, 3))
    wg_h = wg_h.reshape(GLA_HEADS, LANES, 2 * dk).astype(BF16)
    bg_h = jnp.transpose(b_gate.reshape(2, GLA_HEADS, dk), (1, 0, 2)).reshape(GLA_HEADS, 1, 2 * dk)
    o = _gla(q, k, v, lr, wg_h, bg_h)
    gh = head_norm_g[None, :]
    return _odd_out(x, o, r, gh, w_out.astype(BF16), norm_post[None, :])


def kernel(x, even_norm_pre, even_norm_post, even_w_in, rg_conv_w, rg_conv_b, rg_gate_w, rg_gate_b,
           rg_lambda, sc_conv_w, even_w_out, odd_norm_pre, odd_norm_post, odd_w_in, gla_w_gate_lr,
           gla_b_gate, gla_norm_g, odd_w_out):
    depth = even_norm_pre.shape[0] + odd_norm_pre.shape[0]
    for layer in range(depth):
        j = layer // 2
        if layer % 2 == 0:
            x = _even_layer(x, even_norm_pre[j], even_norm_post[j], even_w_in[j], rg_conv_w[j],
                            rg_conv_b[j], rg_gate_w[j], rg_gate_b[j], rg_lambda[j], sc_conv_w[j],
                            even_w_out[j])
        else:
            x = _odd_layer(x, odd_norm_pre[j], odd_norm_post[j], odd_w_in[j], gla_w_gate_lr[j],
                           gla_b_gate[j], gla_norm_g[j], odd_w_out[j])
    return x
```

```python
import functools

import numpy as np
import jax
import jax.numpy as jnp
from jax import lax
from jax.experimental import pallas as pl
from jax.experimental.pallas import tpu as pltpu

F32 = jnp.float32
BF16 = jnp.bfloat16

NORM_EPS = 1e-6
LOG2_E = 1.4426950408889634
RG_HEADS = 8
RG_C = 8.0
GLA_HEADS = 4
GLA_RANK = 16
GLA_NORMALIZER = 16.0
GLA_CHUNK = 64

SUBLANES = 8
LANES = 128
EVEN_STEPS = 64
PERM_STEPS = 32
HALO_STEPS = 2
HALO_FETCH = 8
HALO = HALO_STEPS * SUBLANES
COL_GROUP = 256
ODD_TILE = 512
GLA_BLOCK = 256
GATE_ROWS = 512
VMEM_LIMIT = 56 * 1024 * 1024


def _rms_scale(x):
    return lax.rsqrt(jnp.mean(x * x, axis=-1, keepdims=True) + NORM_EPS)


def _sigmoid(x):
    return 0.5 * jnp.tanh(0.5 * x) + 0.5


def _silu(x):
    return x * _sigmoid(x)


def _dot(a, b):
    return jnp.dot(a, b, preferred_element_type=F32)


def _const_spec(shape):
    return pl.BlockSpec(shape, lambda *_: (0,) * len(shape), pipeline_mode=pl.Buffered(1))


def _time_major_perm(steps, first, count):
    p = np.zeros((count * SUBLANES, steps * SUBLANES), np.float32)
    for j in range(count):
        for b in range(SUBLANES):
            p[j * SUBLANES + b, b * steps + first + j] = 1.0
    return p


def _even_fwd_kernel(xp_ref, x_ref, xn_ref, perm_ref, permp_ref, permn_ref, gpre_ref, win_ref,
                     cw_ref, cb_ref, gw_ref, gb_ref, lam_ref, scw_ref, wout_ref,
                     yp_ref, pz_ref, summ_ref,
                     hbuf, ext_a, ext_b, ua_buf, g_buf, raw_buf, sza_buf, af_buf, hf_buf, pb_buf,
                     hb_buf, yb_buf, carry_f):
    t = pl.program_id(0)
    nt = pl.num_programs(0)
    nb, steps, width = x_ref.shape
    rows = nb * steps
    prows = nb * PERM_STEPS
    hd = width // RG_HEADS

    gpre = gpre_ref[...]

    def normed(xs):
        return (xs * _rms_scale(xs) * gpre).astype(BF16)

    hp = normed(xp_ref[...].reshape(nb * HALO_FETCH, width))
    hbuf[0:HALO, :] = _dot(permp_ref[...], hp).astype(BF16)
    hn = normed(xn_ref[...].reshape(nb * HALO_FETCH, width))
    hbuf[HALO + rows:, :] = _dot(permn_ref[...], hn).astype(BF16)
    for s in range(steps // PERM_STEPS):
        xs = x_ref[:, s * PERM_STEPS:(s + 1) * PERM_STEPS, :].reshape(prows, width)
        hbuf[HALO + s * prows:HALO + (s + 1) * prows, :] = _dot(perm_ref[...], normed(xs)).astype(BF16)

    top_ok = (t > 0).astype(F32)
    bot_ok = (t < nt - 1).astype(F32)

    def halo_mask(ref):
        ref[0:HALO, :] = ref[0:HALO, :] * top_ok
        ref[HALO + rows:, :] = ref[HALO + rows:, :] * bot_ok

    def tap(ref, shift):
        lo = HALO + shift * SUBLANES
        return ref[lo:lo + rows, :]

    def proj(lhs, group, cs):
        return _dot(lhs, win_ref[:, group * width + cs.start:group * width + cs.stop])

    lam = lam_ref[...]
    half_scale = (-0.5 * RG_C * LOG2_E) * (jnp.maximum(-lam, 0.0) + jnp.log1p(jnp.exp(-jnp.abs(lam))))
    gb = gb_ref[...]
    cw = cw_ref[...]
    scw = scw_ref[...]
    hcur = hbuf[HALO:HALO + rows, :]

    ngroups = width // COL_GROUP

    def cols(c):
        return slice(c * COL_GROUP, (c + 1) * COL_GROUP)

    def project(c):
        cs, slot = cols(c), c % 2
        ext_a[slot] = proj(hbuf[...], 0, cs)
        ext_b[slot] = proj(hbuf[...], 2, cs) * proj(hbuf[...], 4, cs)
        raw_buf[slot, 0] = proj(hcur, 3, cs)
        raw_buf[slot, 1] = proj(hcur, 5, cs)
        raw_buf[slot, 2] = proj(hcur, 1, cs)

    def conv_and_gate_matmuls(c):
        cs, xa = cols(c), ext_a.at[c % 2]
        halo_mask(xa)
        ua = cb_ref[:, cs] + cw[0:1, cs] * tap(xa, -2)
        for k in range(1, 4):
            ua = ua + cw[k:k + 1, cs] * tap(xa, k - 2)
        ua_buf[...] = ua
        for j in range(COL_GROUP // hd):
            g_buf[j] = _dot(ua[:, j * hd:(j + 1) * hd].astype(BF16), gw_ref[cs.start // hd + j])

    def gate_arithmetic(c):
        for j in range(COL_GROUP // hd):
            hs = slice(c * COL_GROUP + j * hd, c * COL_GROUP + (j + 1) * hd)
            half_u = 0.5 * ua_buf[:, j * hd:(j + 1) * hd]
            for d, (a_buf, b_buf) in enumerate(((af_buf, hf_buf), (pb_buf, hb_buf))):
                tr = jnp.tanh(g_buf[j, :, (2 * d) * hd:(2 * d + 1) * hd] + gb[2 * d:2 * d + 1, hs])
                ti = jnp.tanh(g_buf[j, :, (2 * d + 1) * hd:(2 * d + 2) * hd] + gb[2 * d + 1:2 * d + 2, hs])
                a = jnp.exp2(half_scale[d:d + 1, hs] * tr + half_scale[d:d + 1, hs])
                a_buf[:, hs] = a
                y = 1.0 - a * a
                root = jnp.where(y > 0.0, y * lax.rsqrt(y), 0.0)
                b_buf[:, hs] = root * (half_u * ti + half_u)

    def mixer_b(c):
        cs, slot, cx = cols(c), c % 2, ext_b.at[c % 2]
        halo_mask(cx)
        cv = scw[0:1, cs] * tap(cx, -1)
        for k in range(1, 3):
            cv = cv + scw[k:k + 1, cs] * tap(cx, k - 1)
        yb_buf[:, cs] = (raw_buf[slot, 0] * cv * _silu(raw_buf[slot, 1])).astype(BF16)
        sza_buf[:, cs] = _silu(raw_buf[slot, 2])

    project(0)
    conv_and_gate_matmuls(0)
    for c in range(ngroups - 1):
        project(c + 1)
        gate_arithmetic(c)
        mixer_b(c)
        conv_and_gate_matmuls(c + 1)
    mixer_b(ngroups - 1)
    yp_ref[...] = _dot(yb_buf[...], wout_ref[width:, :])
    gate_arithmetic(ngroups - 1)

    @pl.when(t == 0)
    def _():
        carry_f[...] = jnp.zeros_like(carry_f)

    def scan_body(k, carry):
        hf, hb, pb = carry
        rf = pl.ds(pl.multiple_of(k * SUBLANES, SUBLANES), SUBLANES)
        rb = pl.ds(pl.multiple_of((steps - 1 - k) * SUBLANES, SUBLANES), SUBLANES)
        hf = af_buf[rf, :] * hf + hf_buf[rf, :]
        hf_buf[rf, :] = hf
        a = pb_buf[rb, :]
        hb = a * hb + hb_buf[rb, :]
        pb = a * pb
        hb_buf[rb, :] = hb
        pb_buf[rb, :] = pb
        return hf, hb, pb

    zeros = jnp.zeros((SUBLANES, width), F32)
    hf, hb, pb = lax.fori_loop(0, steps, scan_body, (carry_f[...], zeros, zeros + 1.0), unroll=4)
    carry_f[...] = hf
    summ_ref[0] = hb
    summ_ref[1] = pb

    sza = sza_buf[...]
    pz_ref[...] = (pb_buf[...] * sza).astype(BF16)
    ya = ((hf_buf[...] + hb_buf[...]) * sza).astype(BF16)
    yp_ref[...] = yp_ref[...] + _dot(ya, wout_ref[0:width, :])


def _even_fwd(x, gpre, w_in, conv_w, conv_b, gate_w, gate_b, lam, sc_w, w_out):
    nb, seq, width = x.shape
    steps = EVEN_STEPS
    rows = nb * steps
    nt = seq // steps
    fpt = steps // HALO_FETCH
    nf = seq // HALO_FETCH
    ext = rows + 2 * HALO
    hd = width // RG_HEADS
    perm = jnp.asarray(_time_major_perm(PERM_STEPS, 0, PERM_STEPS), BF16)
    perm_prev = jnp.asarray(_time_major_perm(HALO_FETCH, HALO_FETCH - HALO_STEPS, HALO_STEPS), BF16)
    perm_next = jnp.asarray(_time_major_perm(HALO_FETCH, 0, HALO_STEPS), BF16)
    consts = (perm, perm_prev, perm_next, gpre, w_in, conv_w, conv_b, gate_w, gate_b, lam, sc_w, w_out)
    return pl.pallas_call(
        _even_fwd_kernel,
        grid=(nt,),
        in_specs=[
            pl.BlockSpec((nb, HALO_FETCH, width), lambda t: (0, jnp.maximum(t * fpt - 1, 0), 0)),
            pl.BlockSpec((nb, steps, width), lambda t: (0, t, 0)),
            pl.BlockSpec((nb, HALO_FETCH, width), lambda t: (0, jnp.minimum((t + 1) * fpt, nf - 1), 0)),
        ] + [_const_spec(c.shape) for c in consts],
        out_specs=[
            pl.BlockSpec((rows, width), lambda t: (t, 0)),
            pl.BlockSpec((rows, width), lambda t: (t, 0)),
            pl.BlockSpec((None, 2, nb, width), lambda t: (t, 0, 0, 0)),
        ],
        out_shape=[
            jax.ShapeDtypeStruct((seq * nb, width), F32),
            jax.ShapeDtypeStruct((seq * nb, width), BF16),
            jax.ShapeDtypeStruct((nt, 2, nb, width), F32),
        ],
        scratch_shapes=[
            pltpu.VMEM((ext, width), BF16),
            pltpu.VMEM((2, ext, COL_GROUP), F32),
            pltpu.VMEM((2, ext, COL_GROUP), F32),
            pltpu.VMEM((rows, COL_GROUP), F32),
            pltpu.VMEM((COL_GROUP // hd, rows, 4 * hd), F32),
            pltpu.VMEM((2, 3, rows, COL_GROUP), F32),
            pltpu.VMEM((rows, width), F32),
            pltpu.VMEM((rows, width), F32),
            pltpu.VMEM((rows, width), F32),
            pltpu.VMEM((rows, width), F32),
            pltpu.VMEM((rows, width), F32),
            pltpu.VMEM((rows, width), BF16),
            pltpu.VMEM((nb, width), F32),
        ],
        compiler_params=pltpu.CompilerParams(
            dimension_semantics=("arbitrary",), vmem_limit_bytes=VMEM_LIMIT),
        name="even_fwd",
    )(x, x, x, *consts)


def _even_fix_kernel(nproj, x_ref, yp_ref, pz_ref, summ_ref, permt_ref, w1_ref, gpost_ref, *rest):
    nin = nproj + 1 if nproj else 0
    proj_in, out_ref, proj_out, carry_b = rest[:nin], rest[nin], rest[nin + 1:-1], rest[-1]
    k = pl.program_id(0)
    nb, steps, width = x_ref.shape
    prows = nb * PERM_STEPS

    @pl.when(k == 0)
    def _():
        carry_b[...] = jnp.zeros_like(carry_b)

    c = carry_b[...]
    pz = pz_ref[...].astype(F32).reshape(steps, nb, width)
    lhs = (pz * c[None, :, :]).reshape(steps * nb, width).astype(BF16)
    y = yp_ref[...] + _dot(lhs, w1_ref[...])
    z = y * _rms_scale(y) * gpost_ref[...]
    for s in range(steps // PERM_STEPS):
        zs = z[s * prows:(s + 1) * prows, :]
        hi = zs.astype(BF16)
        lo = (zs - hi.astype(F32)).astype(BF16)
        nat = _dot(permt_ref[...], hi) + _dot(permt_ref[...], lo)
        sl = slice(s * PERM_STEPS, (s + 1) * PERM_STEPS)
        out_ref[:, sl, :] = x_ref[:, sl, :] + nat.reshape(nb, PERM_STEPS, width)
    carry_b[...] = summ_ref[0] + summ_ref[1] * c

    if nproj:
        x1 = out_ref[...].reshape(nb * steps, width)
        h = (x1 * _rms_scale(x1) * proj_in[0][...]).astype(BF16)
        for w_ref, o_ref in zip(proj_in[1:], proj_out):
            o_ref[...] = _dot(h, w_ref[...]).astype(o_ref.dtype).reshape(o_ref.shape)


def _even_fix(x, yp, pz, summ, w1, gpost, next_gpre=None, next_weights=(), next_dtypes=()):
    nb, seq, width = x.shape
    steps = EVEN_STEPS
    rows = nb * steps
    nt = seq // steps
    nproj = len(next_weights)
    perm_t = jnp.asarray(_time_major_perm(PERM_STEPS, 0, PERM_STEPS).T, BF16)
    consts = (perm_t, w1, gpost) + ((next_gpre,) + tuple(next_weights) if nproj else ())
    tile = lambda n: pl.BlockSpec((nb, steps, n), lambda k: (0, nt - 1 - k, 0))
    widths = [width] + [w.shape[1] for w in next_weights]
    dtypes = [F32] + list(next_dtypes)
    outs = pl.pallas_call(
        functools.partial(_even_fix_kernel, nproj),
        grid=(nt,),
        in_specs=[
            tile(width),
            pl.BlockSpec((rows, width), lambda k: (nt - 1 - k, 0)),
            pl.BlockSpec((rows, width), lambda k: (nt - 1 - k, 0)),
            pl.BlockSpec((None, 2, nb, width), lambda k: (nt - 1 - k, 0, 0, 0)),
        ] + [_const_spec(c.shape) for c in consts],
        out_specs=[tile(n) for n in widths],
        out_shape=[jax.ShapeDtypeStruct((nb, seq, n), dt) for n, dt in zip(widths, dtypes)],
        scratch_shapes=[pltpu.VMEM((nb, width), F32)],
        compiler_params=pltpu.CompilerParams(
            dimension_semantics=("arbitrary",), vmem_limit_bytes=VMEM_LIMIT),
        name="even_fix",
    )(x, yp, pz, summ, *consts)
    return outs[0], tuple(outs[1:])


def _odd_proj_kernel(x_ref, gpre_ref, wq_ref, wk_ref, wv_ref, wr_ref, wlr_ref,
                     q_ref, k_ref, v_ref, r_ref, lr_ref):
    x = x_ref[...]
    h = (x * _rms_scale(x) * gpre_ref[...]).astype(BF16)
    q_ref[...] = _dot(h, wq_ref[...]).astype(BF16)
    k_ref[...] = _dot(h, wk_ref[...]).astype(BF16)
    v_ref[...] = _dot(h, wv_ref[...]).astype(BF16)
    r_ref[...] = _dot(h, wr_ref[...]).astype(BF16)
    lr_ref[...] = _dot(h, wlr_ref[...])


def _odd_proj(x, gpre, wq, wk, wv, wr, wlr):
    bsz, seq, width = x.shape
    tile = ODD_TILE
    nt = seq // tile
    row = lambda n: pl.BlockSpec((None, tile, n), lambda b, t: (b, t, 0))
    outs = [(wq.shape[1], BF16), (wk.shape[1], BF16), (wv.shape[1], BF16), (wr.shape[1], BF16),
            (wlr.shape[1], F32)]
    consts = (gpre, wq, wk, wv, wr, wlr)
    return pl.pallas_call(
        _odd_proj_kernel,
        grid=(bsz, nt),
        in_specs=[row(width)] + [_const_spec(c.shape) for c in consts],
        out_specs=[row(n) for n, _ in outs],
        out_shape=[jax.ShapeDtypeStruct((bsz, seq, n), dt) for n, dt in outs],
        compiler_params=pltpu.CompilerParams(
            dimension_semantics=("arbitrary", "arbitrary"), vmem_limit_bytes=VMEM_LIMIT),
        name="odd_proj",
    )(x, *consts)


def _gla_kernel(q_ref, k_ref, v_ref, lr_ref, wg_ref, bg_ref, o_ref, g2, tri, keep, state):
    seq, dk = q_ref.shape
    dv = v_ref.shape[1]
    blk = GLA_BLOCK
    nchunk = blk // GLA_CHUNK
    nblk = seq // blk
    scale = dk ** -0.5

    row = lax.broadcasted_iota(jnp.int32, (blk, blk), 0)
    col = lax.broadcasted_iota(jnp.int32, (blk, blk), 1)
    same_chunk = (row // GLA_CHUNK) == (col // GLA_CHUNK)
    for d in range(2):
        m = jnp.where(same_chunk & ((col >= row) if d else (col <= row)), 1.0, 0.0)
        keep[d] = m
        tri[d] = m.astype(BF16)
    state[...] = jnp.zeros_like(state)

    def gate_body(i, _):
        rows = pl.ds(pl.multiple_of(i * GATE_ROWS, GATE_ROWS), GATE_ROWS)
        z = _dot(lr_ref[rows, :].astype(BF16), wg_ref[...]) + bg_ref[...]
        g = (jnp.minimum(z, 0.0) - jnp.log(1.0 + jnp.exp(-jnp.abs(z)))) * (1.0 / GLA_NORMALIZER)
        hi = g.astype(BF16)
        lo = (g - hi.astype(F32)).astype(BF16)
        for d in range(2):
            g2[rows, (2 * d) * dk:(2 * d + 1) * dk] = hi[:, d * dk:(d + 1) * dk]
            g2[rows, (2 * d + 1) * dk:(2 * d + 2) * dk] = lo[:, d * dk:(d + 1) * dk]
        return 0

    lax.fori_loop(0, seq // GATE_ROWS, gate_body, 0)

    nt_dims = (((1,), (1,)), ((), ()))
    tn_dims = (((0,), (0,)), ((), ()))
    dirs = (0, 1)

    def pair_body(accumulate, j, _):
        rows = [pl.ds(pl.multiple_of(jb * blk, blk), blk) for jb in (j, nblk - 1 - j)]
        cum2 = [_dot(tri[d], g2[rows[d], (2 * d) * dk:(2 * d + 2) * dk]) for d in dirs]
        bc3 = [(c[:, :dk] + c[:, dk:]).reshape(nchunk, GLA_CHUNK, dk) for c in cum2]
        btot3 = [bc3[0][:, GLA_CHUNK - 1:GLA_CHUNK, :], bc3[1][:, 0:1, :]]
        vb = [v_ref[rows[d], :] for d in dirs]
        kst_f = [k_ref[rows[d], :].astype(F32).reshape(bc3[d].shape) * jnp.exp(btot3[d] - bc3[d])
                 for d in dirs]
        k_st = [x.reshape(blk, dk).astype(BF16) for x in kst_f]
        upd = [[lax.dot_general(vb[d][n * GLA_CHUNK:(n + 1) * GLA_CHUNK, :],
                                k_st[d][n * GLA_CHUNK:(n + 1) * GLA_CHUNK, :], tn_dims,
                                preferred_element_type=F32) for n in range(nchunk)]
               for d in dirs]
        k_in = [(kst_f[d] * jnp.exp(-btot3[d])).reshape(blk, dk).astype(BF16) for d in dirs]
        q_in = [(q_ref[rows[d], :].astype(F32) * jnp.exp(bc3[d].reshape(blk, dk)) * scale).astype(BF16)
                for d in dirs]
        scores = [lax.dot_general(q_in[d], k_in[d], nt_dims, preferred_element_type=F32)
                  for d in dirs]
        decay = [jnp.exp(b) for b in btot3]
        entry = []
        for d in dirs:
            order = range(nchunk - 1, -1, -1) if d else range(nchunk)
            st = state[d]
            ent = [None] * nchunk
            for n in order:
                ent[n] = st.astype(BF16)
                st = st * decay[d][n] + upd[d][n]
            state[d] = st
            entry.append(ent)
        p = [jnp.where(keep[d] > 0.0, scores[d], 0.0).astype(BF16) for d in dirs]
        o_blk = [_dot(p[d], vb[d]) for d in dirs]
        for d in dirs:
            parts = [lax.dot_general(q_in[d][n * GLA_CHUNK:(n + 1) * GLA_CHUNK, :], entry[d][n],
                                     nt_dims, preferred_element_type=F32) for n in range(nchunk)]
            o_new = o_blk[d] + jnp.concatenate(parts, axis=0)
            if accumulate:
                o_ref[rows[d], :] = o_ref[rows[d], :] + o_new
            else:
                o_ref[rows[d], :] = o_new
        return 0

    lax.fori_loop(0, nblk // 2, functools.partial(pair_body, False), 0)
    lax.fori_loop(nblk // 2, nblk, functools.partial(pair_body, True), 0)


def _gla(q, k, v, lr, wg, bg):
    bsz, seq, key = q.shape
    val = v.shape[2]
    dk = key // GLA_HEADS
    dv = val // GLA_HEADS
    return pl.pallas_call(
        _gla_kernel,
        grid=(bsz, GLA_HEADS),
        in_specs=[
            pl.BlockSpec((None, seq, dk), lambda b, h: (b, 0, h)),
            pl.BlockSpec((None, seq, dk), lambda b, h: (b, 0, h)),
            pl.BlockSpec((None, seq, dv), lambda b, h: (b, 0, h)),
            pl.BlockSpec((None, seq, lr.shape[2]), lambda b, h: (b, 0, 0)),
            pl.BlockSpec((None, wg.shape[1], 2 * dk), lambda b, h: (h, 0, 0)),
            pl.BlockSpec((None, 1, 2 * dk), lambda b, h: (h, 0, 0)),
        ],
        out_specs=pl.BlockSpec((None, seq, dv), lambda b, h: (b, 0, h)),
        out_shape=jax.ShapeDtypeStruct((bsz, seq, val), F32),
        scratch_shapes=[
            pltpu.VMEM((seq, 4 * dk), BF16),
            pltpu.VMEM((2, GLA_BLOCK, GLA_BLOCK), BF16),
            pltpu.VMEM((2, GLA_BLOCK, GLA_BLOCK), F32),
            pltpu.VMEM((2, dv, dk), F32),
        ],
        compiler_params=pltpu.CompilerParams(
            dimension_semantics=("arbitrary", "arbitrary"), vmem_limit_bytes=VMEM_LIMIT),
        name="odd_gla",
    )(q, k, v, lr, wg, bg)


def _odd_out_kernel(x_ref, o_ref, r_ref, gh_ref, wout_ref, gpost_ref, out_ref, lhs):
    dv = gh_ref.shape[1]
    gh = gh_ref[...]
    for h in range(GLA_HEADS):
        cs = slice(h * dv, (h + 1) * dv)
        o = o_ref[:, cs]
        gate = _silu(r_ref[:, cs].astype(F32))
        lhs[:, cs] = (o * _rms_scale(o) * gh * gate).astype(BF16)
    y = _dot(lhs[...], wout_ref[...])
    out_ref[...] = x_ref[...] + y * _rms_scale(y) * gpost_ref[...]


def _odd_out(x, o, r, gh, w_out, gpost):
    bsz, seq, width = x.shape
    val = o.shape[2]
    tile = ODD_TILE
    nt = seq // tile
    row = lambda n: pl.BlockSpec((None, tile, n), lambda b, t: (b, t, 0))
    consts = (gh, w_out, gpost)
    return pl.pallas_call(
        _odd_out_kernel,
        grid=(bsz, nt),
        in_specs=[row(width), row(val), row(val)] + [_const_spec(c.shape) for c in consts],
        out_specs=row(width),
        out_shape=jax.ShapeDtypeStruct((bsz, seq, width), F32),
        scratch_shapes=[pltpu.VMEM((tile, val), BF16)],
        compiler_params=pltpu.CompilerParams(
            dimension_semantics=("arbitrary", "arbitrary"), vmem_limit_bytes=VMEM_LIMIT),
        name="odd_out",
    )(x, o, r, *consts)


def _even_layer(x, norm_pre, norm_post, w_in, conv_w, conv_b, gate_w, gate_b, lam, sc_w, w_out,
                next_odd=None):
    bsz, _, width = x.shape
    assert bsz == SUBLANES, "the time-major even layer needs batch == 8"
    hd = width // RG_HEADS
    gw = (0.5 * jnp.transpose(gate_w, (2, 3, 0, 1, 4))).reshape(RG_HEADS, hd, 4 * hd).astype(BF16)
    gb = 0.5 * gate_b.reshape(4, width)
    w_out_b = w_out.astype(BF16)
    yp, pz, summ = _even_fwd(x, norm_pre[None, :], w_in.astype(BF16), conv_w, conv_b[None, :],
                             gw, gb, lam, sc_w, w_out_b)
    if next_odd is None:
        return _even_fix(x, yp, pz, summ, w_out_b[:width], norm_post[None, :])
    next_norm_pre, next_w_in, key, val = next_odd
    return _even_fix(x, yp, pz, summ, w_out_b[:width], norm_post[None, :], next_norm_pre[None, :],
                     _odd_proj_weights(next_w_in, key, val), ODD_PROJ_DTYPES)


ODD_PROJ_DTYPES = (BF16, BF16, BF16, BF16, F32)


def _odd_proj_weights(w_in, key, val):
    w_b = w_in.astype(BF16)
    wq, wk = w_b[:, :key], w_b[:, key:2 * key]
    wv, wr = w_b[:, 2 * key:2 * key + val], w_b[:, 2 * key + val:2 * key + 2 * val]
    wlr = jnp.pad(w_b[:, 2 * key + 2 * val:], ((0, 0), (0, LANES - 2 * GLA_RANK)))
    return wq, wk, wv, wr, wlr


def _odd_layer(x, norm_pre, norm_post, w_in, w_gate_lr, b_gate, head_norm_g, w_out, projections=None):
    key = w_gate_lr.shape[-1]
    val = w_out.shape[0]
    if projections is None:
        projections = _odd_proj(x, norm_pre[None, :], *_odd_proj_weights(w_in, key, val))
    q, k, v, r, lr = projections
    wg = jnp.zeros((2, LANES, key), F32)
    wg = wg.at[0, :GLA_RANK].set(w_gate_lr[0]).at[1, GLA_RANK:2 * GLA_RANK].set(w_gate_lr[1])
    dk = key // GLA_HEADS
    wg_h = jnp.transpose(wg.reshape(2, LANES, GLA_HEADS, dk), (2, 1, 0, 3))
    wg_h = wg_h.reshape(GLA_HEADS, LANES, 2 * dk).astype(BF16)
    bg_h = jnp.transpose(b_gate.reshape(2, GLA_HEADS, dk), (1, 0, 2)).reshape(GLA_HEADS, 1, 2 * dk)
    o = _gla(q, k, v, lr, wg_h, bg_h)
    gh = head_norm_g[None, :]
    return _odd_out(x, o, r, gh, w_out.astype(BF16), norm_post[None, :])


def kernel(x, even_norm_pre, even_norm_post, even_w_in, rg_conv_w, rg_conv_b, rg_gate_w, rg_gate_b,
           rg_lambda, sc_conv_w, even_w_out, odd_norm_pre, odd_norm_post, odd_w_in, gla_w_gate_lr,
           gla_b_gate, gla_norm_g, odd_w_out):
    depth = even_norm_pre.shape[0] + odd_norm_pre.shape[0]
    projections = ()
    for layer in range(depth):
        j = layer // 2
        if layer % 2 == 0:
            next_odd = None
            if layer + 1 < depth:
                next_odd = (odd_norm_pre[j], odd_w_in[j], gla_w_gate_lr.shape[-1], odd_w_out.shape[1])
            x, projections = _even_layer(
                x, even_norm_pre[j], even_norm_post[j], even_w_in[j], rg_conv_w[j], rg_conv_b[j],
                rg_gate_w[j], rg_gate_b[j], rg_lambda[j], sc_conv_w[j], even_w_out[j], next_odd)
        else:
            x = _odd_layer(x, odd_norm_pre[j], odd_norm_post[j], odd_w_in[j], gla_w_gate_lr[j],
                           gla_b_gate[j], gla_norm_g[j], odd_w_out[j], projections or None)
            projections = ()
    return x
```

```python
import functools

import numpy as np
import jax
import jax.numpy as jnp
from jax import lax
from jax.experimental import pallas as pl
from jax.experimental.pallas import tpu as pltpu

F32 = jnp.float32
BF16 = jnp.bfloat16

NORM_EPS = 1e-6
LOG2_E = 1.4426950408889634
RG_HEADS = 8
RG_C = 8.0
GLA_HEADS = 4
GLA_RANK = 16
GLA_NORMALIZER = 16.0
GLA_CHUNK = 64

SUBLANES = 8
LANES = 128
EVEN_STEPS = 64
PERM_STEPS = 32
HALO_STEPS = 2
HALO_FETCH = 8
HALO = HALO_STEPS * SUBLANES
COL_GROUP = 256
ODD_TILE = 512
GLA_BLOCK = 256
GATE_ROWS = 512
VMEM_LIMIT = 56 * 1024 * 1024


def _rms_scale(x):
    return lax.rsqrt(jnp.mean(x * x, axis=-1, keepdims=True) + NORM_EPS)


def _sigmoid(x):
    return 0.5 * jnp.tanh(0.5 * x) + 0.5


def _silu(x):
    return x * _sigmoid(x)


def _dot(a, b):
    return jnp.dot(a, b, preferred_element_type=F32)


def _const_spec(shape):
    return pl.BlockSpec(shape, lambda *_: (0,) * len(shape), pipeline_mode=pl.Buffered(1))


def _time_major_perm(steps, first, count):
    p = np.zeros((count * SUBLANES, steps * SUBLANES), np.float32)
    for j in range(count):
        for b in range(SUBLANES):
            p[j * SUBLANES + b, b * steps + first + j] = 1.0
    return p


def _even_fwd_kernel(xp_ref, x_ref, xn_ref, perm_ref, permp_ref, permn_ref, gpre_ref, win_ref,
                     cw_ref, cb_ref, gw_ref, gb_ref, lam_ref, scw_ref, wout_ref,
                     yp_ref, pz_ref, summ_ref,
                     hbuf, ext_a, ext_b, ua_buf, g_buf, raw_buf, sza_buf, af_buf, hf_buf, pb_buf,
                     hb_buf, yb_buf, carry_f):
    t = pl.program_id(0)
    nt = pl.num_programs(0)
    nb, steps, width = x_ref.shape
    rows = nb * steps
    prows = nb * PERM_STEPS
    hd = width // RG_HEADS

    gpre = gpre_ref[...]

    def normed(xs):
        return (xs * _rms_scale(xs) * gpre).astype(BF16)

    hp = normed(xp_ref[...].reshape(nb * HALO_FETCH, width))
    hbuf[0:HALO, :] = _dot(permp_ref[...], hp).astype(BF16)
    hn = normed(xn_ref[...].reshape(nb * HALO_FETCH, width))
    hbuf[HALO + rows:, :] = _dot(permn_ref[...], hn).astype(BF16)
    for s in range(steps // PERM_STEPS):
        xs = x_ref[:, s * PERM_STEPS:(s + 1) * PERM_STEPS, :].reshape(prows, width)
        hbuf[HALO + s * prows:HALO + (s + 1) * prows, :] = _dot(perm_ref[...], normed(xs)).astype(BF16)

    top_ok = (t > 0).astype(F32)
    bot_ok = (t < nt - 1).astype(F32)

    def halo_mask(ref):
        ref[0:HALO, :] = ref[0:HALO, :] * top_ok
        ref[HALO + rows:, :] = ref[HALO + rows:, :] * bot_ok

    def tap(ref, shift):
        lo = HALO + shift * SUBLANES
        return ref[lo:lo + rows, :]

    def proj(lhs, group, cs):
        return _dot(lhs, win_ref[:, group * width + cs.start:group * width + cs.stop])

    lam = lam_ref[...]
    half_scale = (-0.5 * RG_C * LOG2_E) * (jnp.maximum(-lam, 0.0) + jnp.log1p(jnp.exp(-jnp.abs(lam))))
    gb = gb_ref[...]
    cw = cw_ref[...]
    scw = scw_ref[...]
    hcur = hbuf[HALO:HALO + rows, :]

    ngroups = width // COL_GROUP

    def cols(c):
        return slice(c * COL_GROUP, (c + 1) * COL_GROUP)

    def project(c):
        cs, slot = cols(c), c % 2
        ext_a[slot] = proj(hbuf[...], 0, cs)
        ext_b[slot] = proj(hbuf[...], 2, cs) * proj(hbuf[...], 4, cs)
        raw_buf[slot, 0] = proj(hcur, 3, cs)
        raw_buf[slot, 1] = proj(hcur, 5, cs)
        raw_buf[slot, 2] = proj(hcur, 1, cs)

    def conv_and_gate_matmuls(c):
        cs, xa = cols(c), ext_a.at[c % 2]
        halo_mask(xa)
        ua = cb_ref[:, cs] + cw[0:1, cs] * tap(xa, -2)
        for k in range(1, 4):
            ua = ua + cw[k:k + 1, cs] * tap(xa, k - 2)
        ua_buf[...] = ua
        for j in range(COL_GROUP // hd):
            g_buf[j] = _dot(ua[:, j * hd:(j + 1) * hd].astype(BF16), gw_ref[cs.start // hd + j])

    def gate_arithmetic(c):
        for j in range(COL_GROUP // hd):
            hs = slice(c * COL_GROUP + j * hd, c * COL_GROUP + (j + 1) * hd)
            half_u = 0.5 * ua_buf[:, j * hd:(j + 1) * hd]
            for d, (a_buf, b_buf) in enumerate(((af_buf, hf_buf), (pb_buf, hb_buf))):
                tr = jnp.tanh(g_buf[j, :, (2 * d) * hd:(2 * d + 1) * hd] + gb[2 * d:2 * d + 1, hs])
                ti = jnp.tanh(g_buf[j, :, (2 * d + 1) * hd:(2 * d + 2) * hd] + gb[2 * d + 1:2 * d + 2, hs])
                a = jnp.exp2(half_scale[d:d + 1, hs] * tr + half_scale[d:d + 1, hs])
                a_buf[:, hs] = a
                y = 1.0 - a * a
                root = jnp.where(y > 0.0, y * lax.rsqrt(y), 0.0)
                b_buf[:, hs] = root * (half_u * ti + half_u)

    def mixer_b(c):
        cs, slot, cx = cols(c), c % 2, ext_b.at[c % 2]
        halo_mask(cx)
        cv = scw[0:1, cs] * tap(cx, -1)
        for k in range(1, 3):
            cv = cv + scw[k:k + 1, cs] * tap(cx, k - 1)
        yb_buf[:, cs] = (raw_buf[slot, 0] * cv * _silu(raw_buf[slot, 1])).astype(BF16)
        sza_buf[:, cs] = _silu(raw_buf[slot, 2])

    project(0)
    conv_and_gate_matmuls(0)
    for c in range(ngroups - 1):
        project(c + 1)
        gate_arithmetic(c)
        mixer_b(c)
        conv_and_gate_matmuls(c + 1)
    mixer_b(ngroups - 1)
    yp_ref[...] = _dot(yb_buf[...], wout_ref[width:, :])
    gate_arithmetic(ngroups - 1)

    @pl.when(t == 0)
    def _():
        carry_f[...] = jnp.zeros_like(carry_f)

    def scan_body(k, carry):
        hf, hb, pb = carry
        rf = pl.ds(pl.multiple_of(k * SUBLANES, SUBLANES), SUBLANES)
        rb = pl.ds(pl.multiple_of((steps - 1 - k) * SUBLANES, SUBLANES), SUBLANES)
        hf = af_buf[rf, :] * hf + hf_buf[rf, :]
        hf_buf[rf, :] = hf
        a = pb_buf[rb, :]
        hb = a * hb + hb_buf[rb, :]
        pb = a * pb
        hb_buf[rb, :] = hb
        pb_buf[rb, :] = pb
        return hf, hb, pb

    zeros = jnp.zeros((SUBLANES, width), F32)
    hf, hb, pb = lax.fori_loop(0, steps, scan_body, (carry_f[...], zeros, zeros + 1.0), unroll=True)
    carry_f[...] = hf
    summ_ref[0] = hb
    summ_ref[1] = pb

    sza = sza_buf[...]
    pz_ref[...] = (pb_buf[...] * sza).astype(BF16)
    ya = ((hf_buf[...] + hb_buf[...]) * sza).astype(BF16)
    yp_ref[...] = yp_ref[...] + _dot(ya, wout_ref[0:width, :])


def _even_fwd(x, gpre, w_in, conv_w, conv_b, gate_w, gate_b, lam, sc_w, w_out):
    nb, seq, width = x.shape
    steps = EVEN_STEPS
    rows = nb * steps
    nt = seq // steps
    fpt = steps // HALO_FETCH
    nf = seq // HALO_FETCH
    ext = rows + 2 * HALO
    hd = width // RG_HEADS
    perm = jnp.asarray(_time_major_perm(PERM_STEPS, 0, PERM_STEPS), BF16)
    perm_prev = jnp.asarray(_time_major_perm(HALO_FETCH, HALO_FETCH - HALO_STEPS, HALO_STEPS), BF16)
    perm_next = jnp.asarray(_time_major_perm(HALO_FETCH, 0, HALO_STEPS), BF16)
    consts = (perm, perm_prev, perm_next, gpre, w_in, conv_w, conv_b, gate_w, gate_b, lam, sc_w, w_out)
    return pl.pallas_call(
        _even_fwd_kernel,
        grid=(nt,),
        in_specs=[
            pl.BlockSpec((nb, HALO_FETCH, width), lambda t: (0, jnp.maximum(t * fpt - 1, 0), 0)),
            pl.BlockSpec((nb, steps, width), lambda t: (0, t, 0)),
            pl.BlockSpec((nb, HALO_FETCH, width), lambda t: (0, jnp.minimum((t + 1) * fpt, nf - 1), 0)),
        ] + [_const_spec(c.shape) for c in consts],
        out_specs=[
            pl.BlockSpec((rows, width), lambda t: (t, 0)),
            pl.BlockSpec((rows, width), lambda t: (t, 0)),
            pl.BlockSpec((None, 2, nb, width), lambda t: (t, 0, 0, 0)),
        ],
        out_shape=[
            jax.ShapeDtypeStruct((seq * nb, width), F32),
            jax.ShapeDtypeStruct((seq * nb, width), BF16),
            jax.ShapeDtypeStruct((nt, 2, nb, width), F32),
        ],
        scratch_shapes=[
            pltpu.VMEM((ext, width), BF16),
            pltpu.VMEM((2, ext, COL_GROUP), F32),
            pltpu.VMEM((2, ext, COL_GROUP), F32),
            pltpu.VMEM((rows, COL_GROUP), F32),
            pltpu.VMEM((COL_GROUP // hd, rows, 4 * hd), F32),
            pltpu.VMEM((2, 3, rows, COL_GROUP), F32),
            pltpu.VMEM((rows, width), F32),
            pltpu.VMEM((rows, width), F32),
            pltpu.VMEM((rows, width), F32),
            pltpu.VMEM((rows, width), F32),
            pltpu.VMEM((rows, width), F32),
            pltpu.VMEM((rows, width), BF16),
            pltpu.VMEM((nb, width), F32),
        ],
        compiler_params=pltpu.CompilerParams(
            dimension_semantics=("arbitrary",), vmem_limit_bytes=VMEM_LIMIT),
        name="even_fwd",
    )(x, x, x, *consts)


def _even_fix_kernel(nproj, x_ref, yp_ref, pz_ref, summ_ref, permt_ref, w1_ref, gpost_ref, *rest):
    nin = nproj + 1 if nproj else 0
    proj_in, out_ref, proj_out, carry_b = rest[:nin], rest[nin], rest[nin + 1:-1], rest[-1]
    k = pl.program_id(0)
    nb, steps, width = x_ref.shape
    prows = nb * PERM_STEPS

    @pl.when(k == 0)
    def _():
        carry_b[...] = jnp.zeros_like(carry_b)

    c = carry_b[...]
    pz = pz_ref[...].astype(F32).reshape(steps, nb, width)
    lhs = (pz * c[None, :, :]).reshape(steps * nb, width).astype(BF16)
    y = yp_ref[...] + _dot(lhs, w1_ref[...])
    z = y * _rms_scale(y) * gpost_ref[...]
    for s in range(steps // PERM_STEPS):
        zs = z[s * prows:(s + 1) * prows, :]
        hi = zs.astype(BF16)
        lo = (zs - hi.astype(F32)).astype(BF16)
        nat = _dot(permt_ref[...], hi) + _dot(permt_ref[...], lo)
        sl = slice(s * PERM_STEPS, (s + 1) * PERM_STEPS)
        out_ref[:, sl, :] = x_ref[:, sl, :] + nat.reshape(nb, PERM_STEPS, width)
    carry_b[...] = summ_ref[0] + summ_ref[1] * c

    if nproj:
        x1 = out_ref[...].reshape(nb * steps, width)
        h = (x1 * _rms_scale(x1) * proj_in[0][...]).astype(BF16)
        for w_ref, o_ref in zip(proj_in[1:], proj_out):
            o_ref[...] = _dot(h, w_ref[...]).astype(o_ref.dtype).reshape(o_ref.shape)


def _even_fix(x, yp, pz, summ, w1, gpost, next_gpre=None, next_weights=(), next_dtypes=()):
    nb, seq, width = x.shape
    steps = EVEN_STEPS
    rows = nb * steps
    nt = seq // steps
    nproj = len(next_weights)
    perm_t = jnp.asarray(_time_major_perm(PERM_STEPS, 0, PERM_STEPS).T, BF16)
    consts = (perm_t, w1, gpost) + ((next_gpre,) + tuple(next_weights) if nproj else ())
    tile = lambda n: pl.BlockSpec((nb, steps, n), lambda k: (0, nt - 1 - k, 0))
    widths = [width] + [w.shape[1] for w in next_weights]
    dtypes = [F32] + list(next_dtypes)
    outs = pl.pallas_call(
        functools.partial(_even_fix_kernel, nproj),
        grid=(nt,),
        in_specs=[
            tile(width),
            pl.BlockSpec((rows, width), lambda k: (nt - 1 - k, 0)),
            pl.BlockSpec((rows, width), lambda k: (nt - 1 - k, 0)),
            pl.BlockSpec((None, 2, nb, width), lambda k: (nt - 1 - k, 0, 0, 0)),
        ] + [_const_spec(c.shape) for c in consts],
        out_specs=[tile(n) for n in widths],
        out_shape=[jax.ShapeDtypeStruct((nb, seq, n), dt) for n, dt in zip(widths, dtypes)],
        scratch_shapes=[pltpu.VMEM((nb, width), F32)],
        compiler_params=pltpu.CompilerParams(
            dimension_semantics=("arbitrary",), vmem_limit_bytes=VMEM_LIMIT),
        name="even_fix",
    )(x, yp, pz, summ, *consts)
    return outs[0], tuple(outs[1:])


def _odd_proj_kernel(x_ref, gpre_ref, wq_ref, wk_ref, wv_ref, wr_ref, wlr_ref,
                     q_ref, k_ref, v_ref, r_ref, lr_ref):
    x = x_ref[...]
    h = (x * _rms_scale(x) * gpre_ref[...]).astype(BF16)
    q_ref[...] = _dot(h, wq_ref[...]).astype(BF16)
    k_ref[...] = _dot(h, wk_ref[...]).astype(BF16)
    v_ref[...] = _dot(h, wv_ref[...]).astype(BF16)
    r_ref[...] = _dot(h, wr_ref[...]).astype(BF16)
    lr_ref[...] = _dot(h, wlr_ref[...])


def _odd_proj(x, gpre, wq, wk, wv, wr, wlr):
    bsz, seq, width = x.shape
    tile = ODD_TILE
    nt = seq // tile
    row = lambda n: pl.BlockSpec((None, tile, n), lambda b, t: (b, t, 0))
    outs = [(wq.shape[1], BF16), (wk.shape[1], BF16), (wv.shape[1], BF16), (wr.shape[1], BF16),
            (wlr.shape[1], F32)]
    consts = (gpre, wq, wk, wv, wr, wlr)
    return pl.pallas_call(
        _odd_proj_kernel,
        grid=(bsz, nt),
        in_specs=[row(width)] + [_const_spec(c.shape) for c in consts],
        out_specs=[row(n) for n, _ in outs],
        out_shape=[jax.ShapeDtypeStruct((bsz, seq, n), dt) for n, dt in outs],
        compiler_params=pltpu.CompilerParams(
            dimension_semantics=("arbitrary", "arbitrary"), vmem_limit_bytes=VMEM_LIMIT),
        name="odd_proj",
    )(x, *consts)


def _gla_kernel(q_ref, k_ref, v_ref, lr_ref, wg_ref, bg_ref, o_ref, g2, tri, keep, state, scaled,
                decays):
    seq, dk = q_ref.shape
    dv = v_ref.shape[1]
    blk = GLA_BLOCK
    nchunk = blk // GLA_CHUNK
    nblk = seq // blk
    scale = dk ** -0.5

    row = lax.broadcasted_iota(jnp.int32, (blk, blk), 0)
    col = lax.broadcasted_iota(jnp.int32, (blk, blk), 1)
    same_chunk = (row // GLA_CHUNK) == (col // GLA_CHUNK)
    for d in range(2):
        m = jnp.where(same_chunk & ((col >= row) if d else (col <= row)), 1.0, 0.0)
        keep[d] = m
        tri[d] = m.astype(BF16)
    state[...] = jnp.zeros_like(state)

    def gate_body(i, _):
        rows = pl.ds(pl.multiple_of(i * GATE_ROWS, GATE_ROWS), GATE_ROWS)
        z = _dot(lr_ref[rows, :].astype(BF16), wg_ref[...]) + bg_ref[...]
        g = (jnp.minimum(z, 0.0) - jnp.log(1.0 + jnp.exp(-jnp.abs(z)))) * (1.0 / GLA_NORMALIZER)
        hi = g.astype(BF16)
        lo = (g - hi.astype(F32)).astype(BF16)
        for d in range(2):
            g2[rows, (2 * d) * dk:(2 * d + 1) * dk] = hi[:, d * dk:(d + 1) * dk]
            g2[rows, (2 * d + 1) * dk:(2 * d + 2) * dk] = lo[:, d * dk:(d + 1) * dk]
        return 0

    lax.fori_loop(0, seq // GATE_ROWS, gate_body, 0)

    nt_dims = (((1,), (1,)), ((), ()))
    tn_dims = (((0,), (0,)), ((), ()))
    dirs = (0, 1)

    def block_rows(j):
        return [pl.ds(pl.multiple_of(jb * blk, blk), blk) for jb in (j, nblk - 1 - j)]

    def prepare(slot, j):
        rows = block_rows(j)
        cum2 = [_dot(tri[d], g2[rows[d], (2 * d) * dk:(2 * d + 2) * dk]) for d in dirs]
        bc3 = [(c[:, :dk] + c[:, dk:]).reshape(nchunk, GLA_CHUNK, dk) for c in cum2]
        btot3 = [bc3[0][:, GLA_CHUNK - 1:GLA_CHUNK, :], bc3[1][:, 0:1, :]]
        for d in dirs:
            kst_f = k_ref[rows[d], :].astype(F32).reshape(bc3[d].shape) * jnp.exp(btot3[d] - bc3[d])
            scaled[slot, d, 0] = (q_ref[rows[d], :].astype(F32) * jnp.exp(bc3[d].reshape(blk, dk))
                                  * scale).astype(BF16)
            scaled[slot, d, 1] = (kst_f * jnp.exp(-btot3[d])).reshape(blk, dk).astype(BF16)
            scaled[slot, d, 2] = kst_f.reshape(blk, dk).astype(BF16)
            decays[slot, d] = jnp.exp(btot3[d]).reshape(nchunk, dk)

    def consume(slot, j, accumulate):
        rows = block_rows(j)
        vb = [v_ref[rows[d], :] for d in dirs]
        q_in = [scaled[slot, d, 0] for d in dirs]
        k_in = [scaled[slot, d, 1] for d in dirs]
        k_st = [scaled[slot, d, 2] for d in dirs]
        decay = [decays[slot, d] for d in dirs]
        upd = [[lax.dot_general(k_st[d][n * GLA_CHUNK:(n + 1) * GLA_CHUNK, :],
                                vb[d][n * GLA_CHUNK:(n + 1) * GLA_CHUNK, :], tn_dims,
                                preferred_element_type=F32) for n in range(nchunk)]
               for d in dirs]
        scores = [lax.dot_general(q_in[d], k_in[d], nt_dims, preferred_element_type=F32)
                  for d in dirs]
        entry = []
        for d in dirs:
            order = range(nchunk - 1, -1, -1) if d else range(nchunk)
            st = state[d]
            ent = [None] * nchunk
            for n in order:
                ent[n] = st.astype(BF16)
                dec_rows = jnp.broadcast_to(decay[d][n:n + 1, :], (dk, dk)).T
                st = st * jnp.concatenate([dec_rows] * (dv // dk), axis=1) + upd[d][n]
            state[d] = st
            entry.append(ent)
        p = [jnp.where(keep[d] > 0.0, scores[d], 0.0).astype(BF16) for d in dirs]
        o_blk = [_dot(p[d], vb[d]) for d in dirs]
        for d in dirs:
            parts = [_dot(q_in[d][n * GLA_CHUNK:(n + 1) * GLA_CHUNK, :], entry[d][n])
                     for n in range(nchunk)]
            o_new = o_blk[d] + jnp.concatenate(parts, axis=0)
            if accumulate:
                o_ref[rows[d], :] = o_ref[rows[d], :] + o_new
            else:
                o_ref[rows[d], :] = o_new

    def two_steps(accumulate, i, _):
        j = 2 * i
        prepare(1, j + 1)
        consume(0, j, accumulate)
        prepare(0, jnp.minimum(j + 2, nblk - 1))
        consume(1, j + 1, accumulate)
        return 0

    prepare(0, 0)
    lax.fori_loop(0, nblk // 4, functools.partial(two_steps, False), 0)
    lax.fori_loop(nblk // 4, nblk // 2, functools.partial(two_steps, True), 0)


def _gla(q, k, v, lr, wg, bg):
    bsz, seq, key = q.shape
    val = v.shape[2]
    dk = key // GLA_HEADS
    dv = val // GLA_HEADS
    return pl.pallas_call(
        _gla_kernel,
        grid=(bsz, GLA_HEADS),
        in_specs=[
            pl.BlockSpec((None, seq, dk), lambda b, h: (b, 0, h)),
            pl.BlockSpec((None, seq, dk), lambda b, h: (b, 0, h)),
            pl.BlockSpec((None, seq, dv), lambda b, h: (b, 0, h)),
            pl.BlockSpec((None, seq, lr.shape[2]), lambda b, h: (b, 0, 0)),
            pl.BlockSpec((None, wg.shape[1], 2 * dk), lambda b, h: (h, 0, 0)),
            pl.BlockSpec((None, 1, 2 * dk), lambda b, h: (h, 0, 0)),
        ],
        out_specs=pl.BlockSpec((None, seq, dv), lambda b, h: (b, 0, h)),
        out_shape=jax.ShapeDtypeStruct((bsz, seq, val), F32),
        scratch_shapes=[
            pltpu.VMEM((seq, 4 * dk), BF16),
            pltpu.VMEM((2, GLA_BLOCK, GLA_BLOCK), BF16),
            pltpu.VMEM((2, GLA_BLOCK, GLA_BLOCK), F32),
            pltpu.VMEM((2, dk, dv), F32),
            pltpu.VMEM((2, 2, 3, GLA_BLOCK, dk), BF16),
            pltpu.VMEM((2, 2, GLA_BLOCK // GLA_CHUNK, dk), F32),
        ],
        compiler_params=pltpu.CompilerParams(
            dimension_semantics=("arbitrary", "arbitrary"), vmem_limit_bytes=VMEM_LIMIT),
        name="odd_gla",
    )(q, k, v, lr, wg, bg)


def _odd_out_kernel(x_ref, o_ref, r_ref, gh_ref, wout_ref, gpost_ref, out_ref, lhs):
    dv = gh_ref.shape[1]
    gh = gh_ref[...]
    for h in range(GLA_HEADS):
        cs = slice(h * dv, (h + 1) * dv)
        o = o_ref[:, cs]
        gate = _silu(r_ref[:, cs].astype(F32))
        lhs[:, cs] = (o * _rms_scale(o) * gh * gate).astype(BF16)
    y = _dot(lhs[...], wout_ref[...])
    out_ref[...] = x_ref[...] + y * _rms_scale(y) * gpost_ref[...]


def _odd_out(x, o, r, gh, w_out, gpost):
    bsz, seq, width = x.shape
    val = o.shape[2]
    tile = ODD_TILE
    nt = seq // tile
    row = lambda n: pl.BlockSpec((None, tile, n), lambda b, t: (b, t, 0))
    consts = (gh, w_out, gpost)
    return pl.pallas_call(
        _odd_out_kernel,
        grid=(bsz, nt),
        in_specs=[row(width), row(val), row(val)] + [_const_spec(c.shape) for c in consts],
        out_specs=row(width),
        out_shape=jax.ShapeDtypeStruct((bsz, seq, width), F32),
        scratch_shapes=[pltpu.VMEM((tile, val), BF16)],
        compiler_params=pltpu.CompilerParams(
            dimension_semantics=("arbitrary", "arbitrary"), vmem_limit_bytes=VMEM_LIMIT),
        name="odd_out",
    )(x, o, r, *consts)


def _even_layer(x, norm_pre, norm_post, w_in, conv_w, conv_b, gate_w, gate_b, lam, sc_w, w_out,
                next_odd=None):
    bsz, _, width = x.shape
    assert bsz == SUBLANES, "the time-major even layer needs batch == 8"
    hd = width // RG_HEADS
    gw = (0.5 * jnp.transpose(gate_w, (2, 3, 0, 1, 4))).reshape(RG_HEADS, hd, 4 * hd).astype(BF16)
    gb = 0.5 * gate_b.reshape(4, width)
    w_out_b = w_out.astype(BF16)
    yp, pz, summ = _even_fwd(x, norm_pre[None, :], w_in.astype(BF16), conv_w, conv_b[None, :],
                             gw, gb, lam, sc_w, w_out_b)
    if next_odd is None:
        return _even_fix(x, yp, pz, summ, w_out_b[:width], norm_post[None, :])
    next_norm_pre, next_w_in, key, val = next_odd
    return _even_fix(x, yp, pz, summ, w_out_b[:width], norm_post[None, :], next_norm_pre[None, :],
                     _odd_proj_weights(next_w_in, key, val), ODD_PROJ_DTYPES)


ODD_PROJ_DTYPES = (BF16, BF16, BF16, BF16, F32)


def _odd_proj_weights(w_in, key, val):
    w_b = w_in.astype(BF16)
    wq, wk = w_b[:, :key], w_b[:, key:2 * key]
    wv, wr = w_b[:, 2 * key:2 * key + val], w_b[:, 2 * key + val:2 * key + 2 * val]
    wlr = jnp.pad(w_b[:, 2 * key + 2 * val:], ((0, 0), (0, LANES - 2 * GLA_RANK)))
    return wq, wk, wv, wr, wlr


def _odd_layer(x, norm_pre, norm_post, w_in, w_gate_lr, b_gate, head_norm_g, w_out, projections=None):
    key = w_gate_lr.shape[-1]
    val = w_out.shape[0]
    if projections is None:
        projections = _odd_proj(x, norm_pre[None, :], *_odd_proj_weights(w_in, key, val))
    q, k, v, r, lr = projections
    wg = jnp.zeros((2, LANES, key), F32)
    wg = wg.at[0, :GLA_RANK].set(w_gate_lr[0]).at[1, GLA_RANK:2 * GLA_RANK].set(w_gate_lr[1])
    dk = key // GLA_HEADS
    wg_h = jnp.transpose(wg.reshape(2, LANES, GLA_HEADS, dk), (2, 1, 0, 3))
    wg_h = wg_h.reshape(GLA_HEADS, LANES, 2 * dk).astype(BF16)
    bg_h = jnp.transpose(b_gate.reshape(2, GLA_HEADS, dk), (1, 0, 2)).reshape(GLA_HEADS, 1, 2 * dk)
    o = _gla(q, k, v, lr, wg_h, bg_h)
    gh = head_norm_g[None, :]
    return _odd_out(x, o, r, gh, w_out.astype(BF16), norm_post[None, :])


def kernel(x, even_norm_pre, even_norm_post, even_w_in, rg_conv_w, rg_conv_b, rg_gate_w, rg_gate_b,
           rg_lambda, sc_conv_w, even_w_out, odd_norm_pre, odd_norm_post, odd_w_in, gla_w_gate_lr,
           gla_b_gate, gla_norm_g, odd_w_out):
    depth = even_norm_pre.shape[0] + odd_norm_pre.shape[0]
    projections = ()
    for layer in range(depth):
        j = layer // 2
        if layer % 2 == 0:
            next_odd = None
            if layer + 1 < depth:
                next_odd = (odd_norm_pre[j], odd_w_in[j], gla_w_gate_lr.shape[-1], odd_w_out.shape[1])
            x, projections = _even_layer(
                x, even_norm_pre[j], even_norm_post[j], even_w_in[j], rg_conv_w[j], rg_conv_b[j],
                rg_gate_w[j], rg_gate_b[j], rg_lambda[j], sc_conv_w[j], even_w_out[j], next_odd)
        else:
            x = _odd_layer(x, odd_norm_pre[j], odd_norm_post[j], odd_w_in[j], gla_w_gate_lr[j],
                           gla_b_gate[j], gla_norm_g[j], odd_w_out[j], projections or None)
            projections = ()
    return x
```

```python
import functools

import numpy as np
import jax
import jax.numpy as jnp
from jax import lax
from jax.experimental import pallas as pl
from jax.experimental.pallas import tpu as pltpu

F32 = jnp.float32
BF16 = jnp.bfloat16

NORM_EPS = 1e-6
LOG2_E = 1.4426950408889634
RG_HEADS = 8
RG_C = 8.0
GLA_HEADS = 4
GLA_RANK = 16
GLA_NORMALIZER = 16.0
GLA_CHUNK = 64

SUBLANES = 8
LANES = 128
EVEN_STEPS = 64
PERM_STEPS = 32
HALO_STEPS = 2
HALO_FETCH = 8
HALO = HALO_STEPS * SUBLANES
COL_GROUP = 256
ODD_TILE = 512
GLA_BLOCK = 256
VMEM_LIMIT = 56 * 1024 * 1024


def _rms_scale(x):
    return lax.rsqrt(jnp.mean(x * x, axis=-1, keepdims=True) + NORM_EPS)


def _sigmoid(x):
    return 0.5 * jnp.tanh(0.5 * x) + 0.5


def _silu(x):
    return x * _sigmoid(x)


def _dot(a, b):
    return jnp.dot(a, b, preferred_element_type=F32)


def _const_spec(shape):
    return pl.BlockSpec(shape, lambda *_: (0,) * len(shape), pipeline_mode=pl.Buffered(1))


def _time_major_perm(steps, first, count):
    p = np.zeros((count * SUBLANES, steps * SUBLANES), np.float32)
    for j in range(count):
        for b in range(SUBLANES):
            p[j * SUBLANES + b, b * steps + first + j] = 1.0
    return p


def _even_fwd_kernel(xp_ref, x_ref, xn_ref, perm_ref, permp_ref, permn_ref, gpre_ref, win_ref,
                     cw_ref, cb_ref, gw_ref, gb_ref, lam_ref, scw_ref, wout_ref,
                     yp_ref, pz_ref, summ_ref,
                     hbuf, ext_a, ext_b, ua_buf, g_buf, raw_buf, sza_buf, af_buf, hf_buf, pb_buf,
                     hb_buf, yb_buf, carry_f):
    t = pl.program_id(0)
    nt = pl.num_programs(0)
    nb, steps, width = x_ref.shape
    rows = nb * steps
    prows = nb * PERM_STEPS
    hd = width // RG_HEADS

    gpre = gpre_ref[...]

    def normed(xs):
        return (xs * _rms_scale(xs) * gpre).astype(BF16)

    hp = normed(xp_ref[...].reshape(nb * HALO_FETCH, width))
    hbuf[0:HALO, :] = _dot(permp_ref[...], hp).astype(BF16)
    hn = normed(xn_ref[...].reshape(nb * HALO_FETCH, width))
    hbuf[HALO + rows:, :] = _dot(permn_ref[...], hn).astype(BF16)
    for s in range(steps // PERM_STEPS):
        xs = x_ref[:, s * PERM_STEPS:(s + 1) * PERM_STEPS, :].reshape(prows, width)
        hbuf[HALO + s * prows:HALO + (s + 1) * prows, :] = _dot(perm_ref[...], normed(xs)).astype(BF16)

    top_ok = (t > 0).astype(F32)
    bot_ok = (t < nt - 1).astype(F32)

    def halo_mask(ref):
        ref[0:HALO, :] = ref[0:HALO, :] * top_ok
        ref[HALO + rows:, :] = ref[HALO + rows:, :] * bot_ok

    def tap(ref, shift):
        lo = HALO + shift * SUBLANES
        return ref[lo:lo + rows, :]

    def proj(lhs, group, cs):
        return _dot(lhs, win_ref[:, group * width + cs.start:group * width + cs.stop])

    lam = lam_ref[...]
    half_scale = (-0.5 * RG_C * LOG2_E) * (jnp.maximum(-lam, 0.0) + jnp.log1p(jnp.exp(-jnp.abs(lam))))
    gb = gb_ref[...]
    cw = cw_ref[...]
    scw = scw_ref[...]
    hcur = hbuf[HALO:HALO + rows, :]

    ngroups = width // COL_GROUP

    def cols(c):
        return slice(c * COL_GROUP, (c + 1) * COL_GROUP)

    def project(c):
        cs, slot = cols(c), c % 2
        ext_a[slot] = proj(hbuf[...], 0, cs)
        ext_b[slot] = proj(hbuf[...], 2, cs) * proj(hbuf[...], 4, cs)
        raw_buf[slot, 0] = proj(hcur, 3, cs)
        raw_buf[slot, 1] = proj(hcur, 5, cs)
        raw_buf[slot, 2] = proj(hcur, 1, cs)

    def conv_and_gate_matmuls(c):
        cs, xa = cols(c), ext_a.at[c % 2]
        halo_mask(xa)
        ua = cb_ref[:, cs] + cw[0:1, cs] * tap(xa, -2)
        for k in range(1, 4):
            ua = ua + cw[k:k + 1, cs] * tap(xa, k - 2)
        ua_buf[...] = ua
        for j in range(COL_GROUP // hd):
            g_buf[j] = _dot(ua[:, j * hd:(j + 1) * hd].astype(BF16), gw_ref[cs.start // hd + j])

    def gate_arithmetic(c):
        for j in range(COL_GROUP // hd):
            hs = slice(c * COL_GROUP + j * hd, c * COL_GROUP + (j + 1) * hd)
            half_u = 0.5 * ua_buf[:, j * hd:(j + 1) * hd]
            for d, (a_buf, b_buf) in enumerate(((af_buf, hf_buf), (pb_buf, hb_buf))):
                tr = jnp.tanh(g_buf[j, :, (2 * d) * hd:(2 * d + 1) * hd] + gb[2 * d:2 * d + 1, hs])
                ti = jnp.tanh(g_buf[j, :, (2 * d + 1) * hd:(2 * d + 2) * hd] + gb[2 * d + 1:2 * d + 2, hs])
                a = jnp.exp2(half_scale[d:d + 1, hs] * tr + half_scale[d:d + 1, hs])
                a_buf[:, hs] = a
                y = 1.0 - a * a
                root = jnp.where(y > 0.0, y * lax.rsqrt(y), 0.0)
                b_buf[:, hs] = root * (half_u * ti + half_u)

    def mixer_b(c):
        cs, slot, cx = cols(c), c % 2, ext_b.at[c % 2]
        halo_mask(cx)
        cv = scw[0:1, cs] * tap(cx, -1)
        for k in range(1, 3):
            cv = cv + scw[k:k + 1, cs] * tap(cx, k - 1)
        yb_buf[:, cs] = (raw_buf[slot, 0] * cv * _silu(raw_buf[slot, 1])).astype(BF16)
        sza_buf[:, cs] = _silu(raw_buf[slot, 2])

    project(0)
    conv_and_gate_matmuls(0)
    for c in range(ngroups - 1):
        project(c + 1)
        gate_arithmetic(c)
        mixer_b(c)
        conv_and_gate_matmuls(c + 1)
    mixer_b(ngroups - 1)
    yp_ref[...] = _dot(yb_buf[...], wout_ref[width:, :])
    gate_arithmetic(ngroups - 1)

    @pl.when(t == 0)
    def _():
        carry_f[...] = jnp.zeros_like(carry_f)

    def scan_body(k, carry):
        hf, hb, pb = carry
        rf = pl.ds(pl.multiple_of(k * SUBLANES, SUBLANES), SUBLANES)
        rb = pl.ds(pl.multiple_of((steps - 1 - k) * SUBLANES, SUBLANES), SUBLANES)
        hf = af_buf[rf, :] * hf + hf_buf[rf, :]
        hf_buf[rf, :] = hf
        a = pb_buf[rb, :]
        hb = a * hb + hb_buf[rb, :]
        pb = a * pb
        hb_buf[rb, :] = hb
        pb_buf[rb, :] = pb
        return hf, hb, pb

    zeros = jnp.zeros((SUBLANES, width), F32)
    hf, hb, pb = lax.fori_loop(0, steps, scan_body, (carry_f[...], zeros, zeros + 1.0), unroll=4)
    carry_f[...] = hf
    summ_ref[0] = hb
    summ_ref[1] = pb

    sza = sza_buf[...]
    pz_ref[...] = (pb_buf[...] * sza).astype(BF16)
    ya = ((hf_buf[...] + hb_buf[...]) * sza).astype(BF16)
    yp_ref[...] = yp_ref[...] + _dot(ya, wout_ref[0:width, :])


def _even_fwd(x, gpre, w_in, conv_w, conv_b, gate_w, gate_b, lam, sc_w, w_out):
    nb, seq, width = x.shape
    steps = EVEN_STEPS
    rows = nb * steps
    nt = seq // steps
    fpt = steps // HALO_FETCH
    nf = seq // HALO_FETCH
    ext = rows + 2 * HALO
    hd = width // RG_HEADS
    perm = jnp.asarray(_time_major_perm(PERM_STEPS, 0, PERM_STEPS), BF16)
    perm_prev = jnp.asarray(_time_major_perm(HALO_FETCH, HALO_FETCH - HALO_STEPS, HALO_STEPS), BF16)
    perm_next = jnp.asarray(_time_major_perm(HALO_FETCH, 0, HALO_STEPS), BF16)
    consts = (perm, perm_prev, perm_next, gpre, w_in, conv_w, conv_b, gate_w, gate_b, lam, sc_w, w_out)
    return pl.pallas_call(
        _even_fwd_kernel,
        grid=(nt,),
        in_specs=[
            pl.BlockSpec((nb, HALO_FETCH, width), lambda t: (0, jnp.maximum(t * fpt - 1, 0), 0)),
            pl.BlockSpec((nb, steps, width), lambda t: (0, t, 0)),
            pl.BlockSpec((nb, HALO_FETCH, width), lambda t: (0, jnp.minimum((t + 1) * fpt, nf - 1), 0)),
        ] + [_const_spec(c.shape) for c in consts],
        out_specs=[
            pl.BlockSpec((rows, width), lambda t: (t, 0)),
            pl.BlockSpec((rows, width), lambda t: (t, 0)),
            pl.BlockSpec((None, 2, nb, width), lambda t: (t, 0, 0, 0)),
        ],
        out_shape=[
            jax.ShapeDtypeStruct((seq * nb, width), F32),
            jax.ShapeDtypeStruct((seq * nb, width), BF16),
            jax.ShapeDtypeStruct((nt, 2, nb, width), F32),
        ],
        scratch_shapes=[
            pltpu.VMEM((ext, width), BF16),
            pltpu.VMEM((2, ext, COL_GROUP), F32),
            pltpu.VMEM((2, ext, COL_GROUP), F32),
            pltpu.VMEM((rows, COL_GROUP), F32),
            pltpu.VMEM((COL_GROUP // hd, rows, 4 * hd), F32),
            pltpu.VMEM((2, 3, rows, COL_GROUP), F32),
            pltpu.VMEM((rows, width), F32),
            pltpu.VMEM((rows, width), F32),
            pltpu.VMEM((rows, width), F32),
            pltpu.VMEM((rows, width), F32),
            pltpu.VMEM((rows, width), F32),
            pltpu.VMEM((rows, width), BF16),
            pltpu.VMEM((nb, width), F32),
        ],
        compiler_params=pltpu.CompilerParams(
            dimension_semantics=("arbitrary",), vmem_limit_bytes=VMEM_LIMIT),
        name="even_fwd",
    )(x, x, x, *consts)


def _even_fix_kernel(nproj, x_ref, yp_ref, pz_ref, summ_ref, permt_ref, w1_ref, gpost_ref, *rest):
    nin = nproj + 1 if nproj else 0
    proj_in, out_ref, proj_out, carry_b = rest[:nin], rest[nin], rest[nin + 1:-1], rest[-1]
    k = pl.program_id(0)
    nb, steps, width = x_ref.shape
    prows = nb * PERM_STEPS

    @pl.when(k == 0)
    def _():
        carry_b[...] = jnp.zeros_like(carry_b)

    c = carry_b[...]
    pz = pz_ref[...].astype(F32).reshape(steps, nb, width)
    lhs = (pz * c[None, :, :]).reshape(steps * nb, width).astype(BF16)
    y = yp_ref[...] + _dot(lhs, w1_ref[...])
    z = y * _rms_scale(y) * gpost_ref[...]
    for s in range(steps // PERM_STEPS):
        zs = z[s * prows:(s + 1) * prows, :]
        hi = zs.astype(BF16)
        lo = (zs - hi.astype(F32)).astype(BF16)
        nat = _dot(permt_ref[...], hi) + _dot(permt_ref[...], lo)
        sl = slice(s * PERM_STEPS, (s + 1) * PERM_STEPS)
        out_ref[:, sl, :] = x_ref[:, sl, :] + nat.reshape(nb, PERM_STEPS, width)
    carry_b[...] = summ_ref[0] + summ_ref[1] * c

    if nproj:
        x1 = out_ref[...].reshape(nb * steps, width)
        h = (x1 * _rms_scale(x1) * proj_in[0][...]).astype(BF16)
        for w_ref, o_ref in zip(proj_in[1:], proj_out):
            o_ref[...] = _dot(h, w_ref[...]).astype(o_ref.dtype).reshape(o_ref.shape)


def _even_fix(x, yp, pz, summ, w1, gpost, next_gpre=None, next_weights=(), next_dtypes=()):
    nb, seq, width = x.shape
    steps = EVEN_STEPS
    rows = nb * steps
    nt = seq // steps
    nproj = len(next_weights)
    perm_t = jnp.asarray(_time_major_perm(PERM_STEPS, 0, PERM_STEPS).T, BF16)
    consts = (perm_t, w1, gpost) + ((next_gpre,) + tuple(next_weights) if nproj else ())
    tile = lambda n: pl.BlockSpec((nb, steps, n), lambda k: (0, nt - 1 - k, 0))
    widths = [width] + [w.shape[1] for w in next_weights]
    dtypes = [F32] + list(next_dtypes)
    outs = pl.pallas_call(
        functools.partial(_even_fix_kernel, nproj),
        grid=(nt,),
        in_specs=[
            tile(width),
            pl.BlockSpec((rows, width), lambda k: (nt - 1 - k, 0)),
            pl.BlockSpec((rows, width), lambda k: (nt - 1 - k, 0)),
            pl.BlockSpec((None, 2, nb, width), lambda k: (nt - 1 - k, 0, 0, 0)),
        ] + [_const_spec(c.shape) for c in consts],
        out_specs=[tile(n) for n in widths],
        out_shape=[jax.ShapeDtypeStruct((nb, seq, n), dt) for n, dt in zip(widths, dtypes)],
        scratch_shapes=[pltpu.VMEM((nb, width), F32)],
        compiler_params=pltpu.CompilerParams(
            dimension_semantics=("arbitrary",), vmem_limit_bytes=VMEM_LIMIT),
        name="even_fix",
    )(x, yp, pz, summ, *consts)
    return outs[0], tuple(outs[1:])


def _odd_proj_kernel(x_ref, gpre_ref, wq_ref, wk_ref, wv_ref, wr_ref, wlr_ref,
                     q_ref, k_ref, v_ref, r_ref, lr_ref):
    x = x_ref[...]
    h = (x * _rms_scale(x) * gpre_ref[...]).astype(BF16)
    q_ref[...] = _dot(h, wq_ref[...]).astype(BF16)
    k_ref[...] = _dot(h, wk_ref[...]).astype(BF16)
    v_ref[...] = _dot(h, wv_ref[...]).astype(BF16)
    r_ref[...] = _dot(h, wr_ref[...]).astype(BF16)
    lr_ref[...] = _dot(h, wlr_ref[...])


def _odd_proj(x, gpre, wq, wk, wv, wr, wlr):
    bsz, seq, width = x.shape
    tile = ODD_TILE
    nt = seq // tile
    row = lambda n: pl.BlockSpec((None, tile, n), lambda b, t: (b, t, 0))
    outs = [(wq.shape[1], BF16), (wk.shape[1], BF16), (wv.shape[1], BF16), (wr.shape[1], BF16),
            (wlr.shape[1], F32)]
    consts = (gpre, wq, wk, wv, wr, wlr)
    return pl.pallas_call(
        _odd_proj_kernel,
        grid=(bsz, nt),
        in_specs=[row(width)] + [_const_spec(c.shape) for c in consts],
        out_specs=[row(n) for n, _ in outs],
        out_shape=[jax.ShapeDtypeStruct((bsz, seq, n), dt) for n, dt in outs],
        compiler_params=pltpu.CompilerParams(
            dimension_semantics=("arbitrary", "arbitrary"), vmem_limit_bytes=VMEM_LIMIT),
        name="odd_proj",
    )(x, *consts)


def _gla_kernel(q_ref, k_ref, v_ref, lr_ref, wg_ref, bg_ref, o_ref, o_acc, tri, keep, state, g2,
                scaled, decays):
    seq, dk = q_ref.shape
    dv = v_ref.shape[1]
    blk = GLA_BLOCK
    nchunk = blk // GLA_CHUNK
    nblk = seq // blk
    scale = dk ** -0.5

    row = lax.broadcasted_iota(jnp.int32, (blk, blk), 0)
    col = lax.broadcasted_iota(jnp.int32, (blk, blk), 1)
    same_chunk = (row // GLA_CHUNK) == (col // GLA_CHUNK)
    for d in range(2):
        m = jnp.where(same_chunk & ((col >= row) if d else (col <= row)), 1.0, 0.0)
        keep[d] = m
        tri[d] = m.astype(BF16)
    state[...] = jnp.zeros_like(state)

    nt_dims = (((1,), (1,)), ((), ()))
    tn_dims = (((0,), (0,)), ((), ()))
    dirs = (0, 1)

    def block_rows(j):
        return [pl.ds(pl.multiple_of(jb * blk, blk), blk) for jb in (j, nblk - 1 - j)]

    def gates(slot, j):
        rows = block_rows(j)
        for d in dirs:
            z = (_dot(lr_ref[rows[d], :].astype(BF16), wg_ref[:, d * dk:(d + 1) * dk])
                 + bg_ref[:, d * dk:(d + 1) * dk])
            g = (jnp.minimum(z, 0.0) - jnp.log(1.0 + jnp.exp(-jnp.abs(z)))) * (1.0 / GLA_NORMALIZER)
            hi = g.astype(BF16)
            g2[slot, d, :, 0:dk] = hi
            g2[slot, d, :, dk:2 * dk] = (g - hi.astype(F32)).astype(BF16)

    def prepare(slot, j):
        rows = block_rows(j)
        cum2 = [_dot(tri[d], g2[slot, d]) for d in dirs]
        bc3 = [(c[:, :dk] + c[:, dk:]).reshape(nchunk, GLA_CHUNK, dk) for c in cum2]
        btot3 = [bc3[0][:, GLA_CHUNK - 1:GLA_CHUNK, :], bc3[1][:, 0:1, :]]
        for d in dirs:
            kst_f = k_ref[rows[d], :].astype(F32).reshape(bc3[d].shape) * jnp.exp(btot3[d] - bc3[d])
            scaled[slot, d, 0] = (q_ref[rows[d], :].astype(F32) * jnp.exp(bc3[d].reshape(blk, dk))
                                  * scale).astype(BF16)
            scaled[slot, d, 1] = (kst_f * jnp.exp(-btot3[d])).reshape(blk, dk).astype(BF16)
            scaled[slot, d, 2] = kst_f.reshape(blk, dk).astype(BF16)
            decays[slot, d] = jnp.exp(btot3[d]).reshape(nchunk, dk)

    def consume(slot, j, accumulate):
        rows = block_rows(j)
        vb = [v_ref[rows[d], :] for d in dirs]
        q_in = [scaled[slot, d, 0] for d in dirs]
        k_in = [scaled[slot, d, 1] for d in dirs]
        k_st = [scaled[slot, d, 2] for d in dirs]
        decay = [decays[slot, d] for d in dirs]
        upd = [[lax.dot_general(k_st[d][n * GLA_CHUNK:(n + 1) * GLA_CHUNK, :],
                                vb[d][n * GLA_CHUNK:(n + 1) * GLA_CHUNK, :], tn_dims,
                                preferred_element_type=F32) for n in range(nchunk)]
               for d in dirs]
        scores = [lax.dot_general(q_in[d], k_in[d], nt_dims, preferred_element_type=F32)
                  for d in dirs]
        entry = []
        for d in dirs:
            order = range(nchunk - 1, -1, -1) if d else range(nchunk)
            st = state[d]
            ent = [None] * nchunk
            for n in order:
                ent[n] = st.astype(BF16)
                dec_rows = jnp.broadcast_to(decay[d][n:n + 1, :], (dk, dk)).T
                st = st * jnp.concatenate([dec_rows] * (dv // dk), axis=1) + upd[d][n]
            state[d] = st
            entry.append(ent)
        p = [jnp.where(keep[d] > 0.0, scores[d], 0.0).astype(BF16) for d in dirs]
        o_blk = [_dot(p[d], vb[d]) for d in dirs]
        for d in dirs:
            parts = [_dot(q_in[d][n * GLA_CHUNK:(n + 1) * GLA_CHUNK, :], entry[d][n])
                     for n in range(nchunk)]
            o_new = o_blk[d] + jnp.concatenate(parts, axis=0)
            if accumulate:
                o_ref[rows[d], :] = (o_acc[rows[d], :] + o_new).astype(o_ref.dtype)
            else:
                o_acc[rows[d], :] = o_new

    def two_steps(accumulate, i, _):
        j = 2 * i
        gates(0, jnp.minimum(j + 2, nblk - 1))
        prepare(1, j + 1)
        consume(0, j, accumulate)
        gates(1, jnp.minimum(j + 3, nblk - 1))
        prepare(0, jnp.minimum(j + 2, nblk - 1))
        consume(1, j + 1, accumulate)
        return 0

    gates(0, 0)
    gates(1, 1)
    prepare(0, 0)
    lax.fori_loop(0, nblk // 4, functools.partial(two_steps, False), 0)
    lax.fori_loop(nblk // 4, nblk // 2, functools.partial(two_steps, True), 0)


def _gla(q, k, v, lr, wg, bg):
    bsz, seq, key = q.shape
    val = v.shape[2]
    dk = key // GLA_HEADS
    dv = val // GLA_HEADS
    return pl.pallas_call(
        _gla_kernel,
        grid=(bsz, GLA_HEADS),
        in_specs=[
            pl.BlockSpec((None, seq, dk), lambda b, h: (b, 0, h)),
            pl.BlockSpec((None, seq, dk), lambda b, h: (b, 0, h)),
            pl.BlockSpec((None, seq, dv), lambda b, h: (b, 0, h)),
            pl.BlockSpec((None, seq, lr.shape[2]), lambda b, h: (b, 0, 0)),
            pl.BlockSpec((None, wg.shape[1], 2 * dk), lambda b, h: (h, 0, 0)),
            pl.BlockSpec((None, 1, 2 * dk), lambda b, h: (h, 0, 0)),
        ],
        out_specs=pl.BlockSpec((None, seq, dv), lambda b, h: (b, 0, h)),
        out_shape=jax.ShapeDtypeStruct((bsz, seq, val), BF16),
        scratch_shapes=[
            pltpu.VMEM((seq, dv), F32),
            pltpu.VMEM((2, GLA_BLOCK, GLA_BLOCK), BF16),
            pltpu.VMEM((2, GLA_BLOCK, GLA_BLOCK), F32),
            pltpu.VMEM((2, dk, dv), F32),
            pltpu.VMEM((2, 2, GLA_BLOCK, 2 * dk), BF16),
            pltpu.VMEM((2, 2, 3, GLA_BLOCK, dk), BF16),
            pltpu.VMEM((2, 2, GLA_BLOCK // GLA_CHUNK, dk), F32),
        ],
        compiler_params=pltpu.CompilerParams(
            dimension_semantics=("arbitrary", "arbitrary"), vmem_limit_bytes=VMEM_LIMIT),
        name="odd_gla",
    )(q, k, v, lr, wg, bg)


def _odd_out_kernel(x_ref, o_ref, r_ref, gh_ref, wout_ref, gpost_ref, out_ref, lhs):
    dv = gh_ref.shape[1]
    gh = gh_ref[...]
    for h in range(GLA_HEADS):
        cs = slice(h * dv, (h + 1) * dv)
        o = o_ref[:, cs].astype(F32)
        gate = _silu(r_ref[:, cs].astype(F32))
        lhs[:, cs] = (o * _rms_scale(o) * gh * gate).astype(BF16)
    y = _dot(lhs[...], wout_ref[...])
    out_ref[...] = x_ref[...] + y * _rms_scale(y) * gpost_ref[...]


def _odd_out(x, o, r, gh, w_out, gpost):
    bsz, seq, width = x.shape
    val = o.shape[2]
    tile = ODD_TILE
    nt = seq // tile
    row = lambda n: pl.BlockSpec((None, tile, n), lambda b, t: (b, t, 0))
    consts = (gh, w_out, gpost)
    return pl.pallas_call(
        _odd_out_kernel,
        grid=(bsz, nt),
        in_specs=[row(width), row(val), row(val)] + [_const_spec(c.shape) for c in consts],
        out_specs=row(width),
        out_shape=jax.ShapeDtypeStruct((bsz, seq, width), F32),
        scratch_shapes=[pltpu.VMEM((tile, val), BF16)],
        compiler_params=pltpu.CompilerParams(
            dimension_semantics=("arbitrary", "arbitrary"), vmem_limit_bytes=VMEM_LIMIT),
        name="odd_out",
    )(x, o, r, *consts)


def _even_layer(x, norm_pre, norm_post, w_in, conv_w, conv_b, gate_w, gate_b, lam, sc_w, w_out,
                next_odd=None):
    bsz, _, width = x.shape
    assert bsz == SUBLANES, "the time-major even layer needs batch == 8"
    hd = width // RG_HEADS
    gw = (0.5 * jnp.transpose(gate_w, (2, 3, 0, 1, 4))).reshape(RG_HEADS, hd, 4 * hd).astype(BF16)
    gb = 0.5 * gate_b.reshape(4, width)
    w_out_b = w_out.astype(BF16)
    yp, pz, summ = _even_fwd(x, norm_pre[None, :], w_in.astype(BF16), conv_w, conv_b[None, :],
                             gw, gb, lam, sc_w, w_out_b)
    if next_odd is None:
        return _even_fix(x, yp, pz, summ, w_out_b[:width], norm_post[None, :])
    next_norm_pre, next_w_in, key, val = next_odd
    return _even_fix(x, yp, pz, summ, w_out_b[:width], norm_post[None, :], next_norm_pre[None, :],
                     _odd_proj_weights(next_w_in, key, val), ODD_PROJ_DTYPES)


ODD_PROJ_DTYPES = (BF16, BF16, BF16, BF16, F32)


def _odd_proj_weights(w_in, key, val):
    w_b = w_in.astype(BF16)
    wq, wk = w_b[:, :key], w_b[:, key:2 * key]
    wv, wr = w_b[:, 2 * key:2 * key + val], w_b[:, 2 * key + val:2 * key + 2 * val]
    wlr = jnp.pad(w_b[:, 2 * key + 2 * val:], ((0, 0), (0, LANES - 2 * GLA_RANK)))
    return wq, wk, wv, wr, wlr


def _odd_layer(x, norm_pre, norm_post, w_in, w_gate_lr, b_gate, head_norm_g, w_out, projections=None):
    key = w_gate_lr.shape[-1]
    val = w_out.shape[0]
    if projections is None:
        projections = _odd_proj(x, norm_pre[None, :], *_odd_proj_weights(w_in, key, val))
    q, k, v, r, lr = projections
    wg = jnp.zeros((2, LANES, key), F32)
    wg = wg.at[0, :GLA_RANK].set(w_gate_lr[0]).at[1, GLA_RANK:2 * GLA_RANK].set(w_gate_lr[1])
    dk = key // GLA_HEADS
    wg_h = jnp.transpose(wg.reshape(2, LANES, GLA_HEADS, dk), (2, 1, 0, 3))
    wg_h = wg_h.reshape(GLA_HEADS, LANES, 2 * dk).astype(BF16)
    bg_h = jnp.transpose(b_gate.reshape(2, GLA_HEADS, dk), (1, 0, 2)).reshape(GLA_HEADS, 1, 2 * dk)
    o = _gla(q, k, v, lr, wg_h, bg_h)
    gh = head_norm_g[None, :]
    return _odd_out(x, o, r, gh, w_out.astype(BF16), norm_post[None, :])


def kernel(x, even_norm_pre, even_norm_post, even_w_in, rg_conv_w, rg_conv_b, rg_gate_w, rg_gate_b,
           rg_lambda, sc_conv_w, even_w_out, odd_norm_pre, odd_norm_post, odd_w_in, gla_w_gate_lr,
           gla_b_gate, gla_norm_g, odd_w_out):
    depth = even_norm_pre.shape[0] + odd_norm_pre.shape[0]
    projections = ()
    for layer in range(depth):
        j = layer // 2
        if layer % 2 == 0:
            next_odd = None
            if layer + 1 < depth:
                next_odd = (odd_norm_pre[j], odd_w_in[j], gla_w_gate_lr.shape[-1], odd_w_out.shape[1])
            x, projections = _even_layer(
                x, even_norm_pre[j], even_norm_post[j], even_w_in[j], rg_conv_w[j], rg_conv_b[j],
                rg_gate_w[j], rg_gate_b[j], rg_lambda[j], sc_conv_w[j], even_w_out[j], next_odd)
        else:
            x = _odd_layer(x, odd_norm_pre[j], odd_norm_post[j], odd_w_in[j], gla_w_gate_lr[j],
                           gla_b_gate[j], gla_norm_g[j], odd_w_out[j], projections or None)
            projections = ()
    return x
```
